```python
import jax, jax.numpy as jnp
from jax import lax
import numpy as np

D_MODEL = 1024
BATCH = 32
SEQ = 2048
DEPTH = 1
DEC_BATCH = 8
DEC_SEQ = 64
PAST_LEN = 1024

CHUNK = 64
N_HEADS_A = 16
N_KV_A = 4
HEAD_DIM_A = 64
GROUP_A = N_HEADS_A // N_KV_A
WINDOW = 128
WINDOW_CHUNKS = WINDOW // CHUNK
BAND = (WINDOW_CHUNKS + 1) * CHUNK
ROT_DIM = HEAD_DIM_A // 4
ROPE_THETA = 500000.0
WIDTH_A = N_HEADS_A * HEAD_DIM_A
KV_WIDTH_A = N_KV_A * HEAD_DIM_A
N_HEADS_B = 8
HEAD_DK_B = 128
HEAD_DV_B = 128
CONV_W = 4
QK_WIDTH_B = N_HEADS_B * HEAD_DK_B
WIDTH_B = N_HEADS_B * HEAD_DV_B
CONV_CH = 2 * QK_WIDTH_B + WIDTH_B
SPLITS = (WIDTH_A, KV_WIDTH_A, KV_WIDTH_A, WIDTH_A, CONV_CH, N_HEADS_B, N_HEADS_B, WIDTH_B, D_MODEL, D_MODEL)
D_IN = sum(SPLITS)
ALPHA = (2.0 * DEPTH) ** 0.25
BETA_INIT = (8.0 * DEPTH) ** -0.25
LN_EPS = 1e-5
RMS_EPS = 1e-6
L2_EPS = 1e-6
NEG_INF = -1e30

kernel_name = "hybrid_swa_sink_gated_deltanet_stream_step"


def in_projection(x, w_in):
    y = jnp.einsum("bsd,de->bse", x, w_in)
    idx = [int(i) for i in np.cumsum(SPLITS)[:-1]]
    return jnp.split(y, idx, axis=-1)


def partial_rope(x, pos):
    half = ROT_DIM // 2
    inv_freq = jnp.power(ROPE_THETA, -jnp.arange(half, dtype=jnp.float32) / half)
    ang = pos[:, None] * inv_freq[None, :]
    cos = jnp.cos(ang)[None, :, None, :]
    sin = jnp.sin(ang)[None, :, None, :]
    xf = x.astype(jnp.float32)
    x1 = xf[..., :half]
    x2 = xf[..., half:ROT_DIM]
    rot = jnp.concatenate([x1 * cos - x2 * sin, x2 * cos + x1 * sin], axis=-1)
    return jnp.concatenate([rot.astype(x.dtype), x[..., ROT_DIM:]], axis=-1)


def sink_softmax(s, sinks):
    m = jnp.maximum(jnp.max(s, axis=-1, keepdims=True), sinks)
    p = jnp.exp(s - m)
    return p / (jnp.sum(p, axis=-1, keepdims=True) + jnp.exp(sinks - m))


def window_attention_prompt(q, k, v, attn_sinks):
    bsz, s_len = q.shape[:2]
    n_c = s_len // CHUNK
    pad = WINDOW_CHUNKS * CHUNK
    kp = jnp.pad(k, ((0, 0), (pad, 0), (0, 0), (0, 0))).reshape(bsz, n_c + WINDOW_CHUNKS, CHUNK, N_KV_A, HEAD_DIM_A)
    vp = jnp.pad(v, ((0, 0), (pad, 0), (0, 0), (0, 0))).reshape(bsz, n_c + WINDOW_CHUNKS, CHUNK, N_KV_A, HEAD_DIM_A)
    kb = jnp.concatenate([kp[:, j:j + n_c] for j in range(WINDOW_CHUNKS + 1)], axis=2)
    vb = jnp.concatenate([vp[:, j:j + n_c] for j in range(WINDOW_CHUNKS + 1)], axis=2)
    qc = q.reshape(bsz, n_c, CHUNK, N_KV_A, GROUP_A, HEAD_DIM_A)
    s = jnp.einsum("bcqkgd,bcjkd->bckgqj", qc, kb).astype(jnp.float32) * (HEAD_DIM_A ** -0.5)
    kpos = jnp.arange(n_c)[:, None] * CHUNK - pad + jnp.arange(BAND)[None, :]
    s = jnp.where((kpos >= 0)[None, :, None, None, None, :], s, NEG_INF)
    sinks = attn_sinks.astype(jnp.float32).reshape(N_KV_A, GROUP_A)[None, None, :, :, None, None]
    p = sink_softmax(s, sinks)
    o = jnp.einsum("bckgqj,bcjkd->bcqkgd", p.astype(v.dtype), vb)
    return o.reshape(bsz, s_len, WIDTH_A)


def window_attention_sample(q, k_all, v_all, attn_sinks):
    bsz, l_len = q.shape[:2]
    qg = q.reshape(bsz, l_len, N_KV_A, GROUP_A, HEAD_DIM_A)
    s = jnp.einsum("bqkgd,bjkd->bkgqj", qg, k_all).astype(jnp.float32) * (HEAD_DIM_A ** -0.5)
    sinks = attn_sinks.astype(jnp.float32).reshape(N_KV_A, GROUP_A)[None, :, :, None, None]
    p = sink_softmax(s, sinks)
    o = jnp.einsum("bkgqj,bjkd->bqkgd", p.astype(v_all.dtype), v_all)
    return o.reshape(bsz, l_len, WIDTH_A)


def causal_conv(u, hist, conv_w):
    ext = jnp.concatenate([hist, u], axis=1)
    out = lax.conv_general_dilated(ext, conv_w[:, None, :].astype(ext.dtype), window_strides=(1,), padding="VALID",
                                   dimension_numbers=("NWC", "WIO", "NWC"), feature_group_count=CONV_CH)
    return jax.nn.silu(out), ext[:, ext.shape[1] - (CONV_W - 1):]


def l2norm(t):
    return t * lax.rsqrt(jnp.sum(t * t, axis=-1, keepdims=True) + L2_EPS)


def gated_delta_chunked(q, k, v, g, beta, s0):
    bsz, nh, l_len, _ = q.shape
    dv = v.shape[-1]
    c = min(CHUNK, l_len)
    n = l_len // c

    def blocks(t):
        return t.reshape((bsz, nh, n, c) + t.shape[3:])

    q, k, v, g, beta = blocks(q), blocks(k), blocks(v), blocks(g), blocks(beta)
    gc = jnp.cumsum(g, axis=-1)
    idx = jnp.arange(c)
    causal = idx[:, None] >= idx[None, :]
    strict = idx[:, None] > idx[None, :]
    diff = gc[..., :, None] - gc[..., None, :]
    decay = jnp.where(causal, jnp.exp(jnp.where(causal, diff, 0.0)), 0.0)
    k_beta = k * beta[..., None]
    m = jnp.where(strict, jnp.einsum("bhnid,bhnjd->bhnij", k_beta, k) * decay, 0.0)
    eye = jnp.eye(c, dtype=jnp.float32)
    t_inv = lax.linalg.triangular_solve(eye + m, jnp.broadcast_to(eye, m.shape), left_side=True, lower=True)
    u = jnp.einsum("bhnij,bhnje->bhnie", t_inv, v * beta[..., None])
    w = jnp.einsum("bhnij,bhnjd->bhnid", t_inv, k_beta * jnp.exp(gc)[..., None])
    qk = jnp.einsum("bhnid,bhnjd->bhnij", q, k) * decay
    q_dec = q * jnp.exp(gc)[..., None]
    k_dec = k * jnp.exp(gc[..., -1:] - gc)[..., None]
    g_last = jnp.exp(gc[..., -1])

    def step(s, blk):
        u_i, w_i, qk_i, q_i, k_i, gl_i = blk
        v_new = u_i - jnp.einsum("bhid,bhde->bhie", w_i, s)
        o_i = jnp.einsum("bhid,bhde->bhie", q_i, s) + jnp.einsum("bhij,bhje->bhie", qk_i, v_new)
        s = s * gl_i[..., None, None] + jnp.einsum("bhid,bhie->bhde", k_i, v_new)
        return s, o_i

    xs = tuple(jnp.moveaxis(t, 2, 0) for t in (u, w, qk, q_dec, k_dec, g_last))
    s_final, o = lax.scan(step, s0, xs)
    o = jnp.moveaxis(o, 0, 2).reshape(bsz, nh, l_len, dv)
    return o, s_final


def gated_rmsnorm(o, z, w):
    bsz, l_len = o.shape[:2]
    o = o * lax.rsqrt(jnp.mean(o * o, axis=-1, keepdims=True) + RMS_EPS) * w.astype(jnp.float32)
    return (o.reshape(bsz, l_len, WIDTH_B) * jax.nn.silu(z.astype(jnp.float32))).astype(z.dtype)


def layer_norm(x, g, b):
    xf = x.astype(jnp.float32)
    mu = jnp.mean(xf, axis=-1, keepdims=True)
    var = jnp.mean(jnp.square(xf - mu), axis=-1, keepdims=True)
    y = (xf - mu) * lax.rsqrt(var + LN_EPS) * g.astype(jnp.float32) + b.astype(jnp.float32)
    return y.astype(x.dtype)


def hybrid_layer(x, pos_offset, k_hist, v_hist, conv_hist, s0, w_in, attn_sinks, conv_w, a_log, dt_bias,
                 delta_norm_w, w_o_attn, w_o_delta, w_out, ln_g, ln_b):
    bsz, l_len, _ = x.shape
    qa, ka, va, za, qkv_b, a_b, b_b, z_b, g_a, g_b = in_projection(x, w_in)
    pos = jnp.arange(l_len, dtype=jnp.float32) + pos_offset
    qa = partial_rope(qa.reshape(bsz, l_len, N_HEADS_A, HEAD_DIM_A), pos)
    ka = partial_rope(ka.reshape(bsz, l_len, N_KV_A, HEAD_DIM_A), pos)
    va = va.reshape(bsz, l_len, N_KV_A, HEAD_DIM_A)
    if k_hist is None:
        oa = window_attention_prompt(qa, ka, va, attn_sinks)
        keep = min(WINDOW, PAST_LEN)
        new_k = ka[:, l_len - keep:]
        new_v = va[:, l_len - keep:]
        conv_hist = jnp.zeros((bsz, CONV_W - 1, CONV_CH), dtype=x.dtype)
        s0 = jnp.zeros((bsz, N_HEADS_B, HEAD_DK_B, HEAD_DV_B), dtype=jnp.float32)
    else:
        k_all = jnp.concatenate([k_hist.astype(ka.dtype), ka], axis=1)
        v_all = jnp.concatenate([v_hist.astype(va.dtype), va], axis=1)
        oa = window_attention_sample(qa, k_all, v_all, attn_sinks)
        keep = k_hist.shape[1]
        new_k = k_all[:, k_all.shape[1] - keep:]
        new_v = v_all[:, v_all.shape[1] - keep:]
    y_a = jnp.einsum("bse,ed->bsd", oa * jax.nn.silu(za), w_o_attn)

    qkv_c, new_conv = causal_conv(qkv_b, conv_hist.astype(qkv_b.dtype), conv_w)
    qb, kb, vb = jnp.split(qkv_c.astype(jnp.float32), [QK_WIDTH_B, 2 * QK_WIDTH_B], axis=-1)
    qb = l2norm(qb.reshape(bsz, l_len, N_HEADS_B, HEAD_DK_B)) * (HEAD_DK_B ** -0.5)
    kb = l2norm(kb.reshape(bsz, l_len, N_HEADS_B, HEAD_DK_B))
    vb = vb.reshape(bsz, l_len, N_HEADS_B, HEAD_DV_B)
    g = -jnp.exp(a_log.astype(jnp.float32)) * jax.nn.softplus(a_b.astype(jnp.float32) + dt_bias.astype(jnp.float32))
    beta = jax.nn.sigmoid(b_b.astype(jnp.float32))
    ob, s_new = gated_delta_chunked(jnp.swapaxes(qb, 1, 2), jnp.swapaxes(kb, 1, 2), jnp.swapaxes(vb, 1, 2),
                                    jnp.swapaxes(g, 1, 2), jnp.swapaxes(beta, 1, 2), s0.astype(jnp.float32))
    ob = gated_rmsnorm(jnp.swapaxes(ob, 1, 2), z_b, delta_norm_w)
    y_b = jnp.einsum("bse,ed->bsd", ob.astype(x.dtype), w_o_delta)

    h = jax.nn.sigmoid(g_a) * y_a + jax.nn.sigmoid(g_b) * y_b
    sub = jnp.einsum("bsd,de->bse", h, w_out)
    y = layer_norm(ALPHA * x + sub, ln_g, ln_b)
    return y, new_k, new_v, new_conv.astype(x.dtype), s_new.astype(x.dtype)


def setup_inputs(seed: int = 0) -> dict:
    key = jax.random.key(seed)
    ks = jax.random.split(key, 20)
    keep = min(WINDOW, PAST_LEN)
    nrm = jax.random.normal
    dt = jnp.exp(jax.random.uniform(ks[10], (N_HEADS_B,), minval=float(np.log(1e-3)), maxval=float(np.log(1e-1))))
    return {
        "x_prompt": nrm(ks[0], (BATCH, SEQ, D_MODEL), jnp.float32),
        "x_sample": nrm(ks[1], (DEC_BATCH, DEC_SEQ, D_MODEL), jnp.float32),
        "cache_attn_k": nrm(ks[2], (DEC_BATCH, keep, N_KV_A, HEAD_DIM_A), jnp.float32),
        "cache_attn_v": nrm(ks[3], (DEC_BATCH, keep, N_KV_A, HEAD_DIM_A), jnp.float32),
        "state_conv": nrm(ks[4], (DEC_BATCH, CONV_W - 1, CONV_CH), jnp.float32),
        "state_delta": nrm(ks[5], (DEC_BATCH, N_HEADS_B, HEAD_DK_B, HEAD_DV_B), jnp.float32) * HEAD_DK_B ** -0.5,
        "w_in": nrm(ks[6], (D_MODEL, D_IN), jnp.float32) * D_MODEL ** -0.5,
        "attn_sinks": nrm(ks[7], (N_HEADS_A,), jnp.float32),
        "conv_w": nrm(ks[8], (CONV_W, CONV_CH), jnp.float32) * CONV_W ** -0.5,
        "a_log": jnp.log(jax.random.uniform(ks[9], (N_HEADS_B,), minval=1.0, maxval=16.0)),
        "dt_bias": dt + jnp.log(-jnp.expm1(-dt)),
        "delta_norm_w": 1.0 + 0.02 * nrm(ks[11], (HEAD_DV_B,), jnp.float32),
        "w_o_attn": nrm(ks[12], (WIDTH_A, D_MODEL), jnp.float32) * WIDTH_A ** -0.5 * BETA_INIT,
        "w_o_delta": nrm(ks[13], (WIDTH_B, D_MODEL), jnp.float32) * WIDTH_B ** -0.5 * BETA_INIT,
        "w_out": nrm(ks[14], (D_MODEL, D_MODEL), jnp.float32) * D_MODEL ** -0.5 * BETA_INIT,
        "ln_g": 1.0 + 0.02 * nrm(ks[15], (D_MODEL,), jnp.float32),
        "ln_b": 0.02 * nrm(ks[16], (D_MODEL,), jnp.float32),
    }


def reference(x_prompt, x_sample, cache_attn_k, cache_attn_v, state_conv, state_delta, w_in, attn_sinks, conv_w,
              a_log, dt_bias, delta_norm_w, w_o_attn, w_o_delta, w_out, ln_g, ln_b):
    yp, yp_new = x_prompt, x_sample
    kp = vp = cp = sp = ks = vs = cs = ss = None
    for _layer in range(DEPTH):
        yp, kp, vp, cp, sp = hybrid_layer(yp, 0.0, None, None, None, None, w_in, attn_sinks, conv_w, a_log,
                                          dt_bias, delta_norm_w, w_o_attn, w_o_delta, w_out, ln_g, ln_b)
        yp_new, ks, vs, cs, ss = hybrid_layer(yp_new, float(PAST_LEN), cache_attn_k, cache_attn_v, state_conv,
                                              state_delta, w_in, attn_sinks, conv_w, a_log, dt_bias, delta_norm_w,
                                              w_o_attn, w_o_delta, w_out, ln_g, ln_b)
    return (yp, yp_new, kp, vp, cp, sp, ks, vs, cs, ss)
```

```python
import functools

import numpy as np
import jax
import jax.numpy as jnp
from jax import lax
from jax.experimental import pallas as pl
from jax.experimental.pallas import tpu as pltpu

D_MODEL = 1024
CHUNK = 64
N_HEADS_A = 16
N_KV_A = 4
HEAD_DIM_A = 64
GROUP_A = N_HEADS_A // N_KV_A
WINDOW = 128
ROT_DIM = HEAD_DIM_A // 4
ROPE_THETA = 500000.0
WIDTH_A = N_HEADS_A * HEAD_DIM_A
KV_WIDTH_A = N_KV_A * HEAD_DIM_A
N_HEADS_B = 8
HEAD_DK_B = 128
HEAD_DV_B = 128
CONV_W = 4
QK_WIDTH_B = N_HEADS_B * HEAD_DK_B
WIDTH_B = N_HEADS_B * HEAD_DV_B
CONV_CH = 2 * QK_WIDTH_B + WIDTH_B
DEPTH = 1
PAST_LEN = 1024
ALPHA = (2.0 * DEPTH) ** 0.25
LN_EPS = 1e-5
RMS_EPS = 1e-6
L2_EPS = 1e-6
NEG_INF = -1e30

LANES = 128
AB_PAD = LANES
OFF_Q = 0
OFF_K = OFF_Q + WIDTH_A
OFF_V = OFF_K + KV_WIDTH_A
OFF_ZA = OFF_V + KV_WIDTH_A
OFF_QKVB = OFF_ZA + WIDTH_A
OFF_ZB = OFF_QKVB + CONV_CH
OFF_GA = OFF_ZB + WIDTH_B
OFF_GB = OFF_GA + D_MODEL
OFF_AB = OFF_GB + D_MODEL
D_IN_PAD = OFF_AB + AB_PAD

VMEM_LIMIT = 56 * 1024 * 1024

F32 = jnp.float32
BF16 = jnp.bfloat16


def _sigmoid(x):
    return 1.0 / (1.0 + jnp.exp(-x))


def _silu(x):
    return x * _sigmoid(x)


def _softplus(x):
    return jnp.maximum(x, 0.0) + jnp.log1p(jnp.exp(-jnp.abs(x)))


def _dot(a, b):
    return jnp.dot(a.astype(BF16), b.astype(BF16), preferred_element_type=F32)


def _dot_nt(a, b):
    return lax.dot_general(a.astype(BF16), b.astype(BF16), (((1,), (1,)), ((), ())),
                           preferred_element_type=F32)


def _rope_cols(t, ct, sa, sb):
    return t * ct + pltpu.roll(t, 8, axis=1) * sa + pltpu.roll(t, LANES - 8, axis=1) * sb


def _proj_kernel(x_ref, w_ref, ct_ref, sa_ref, sb_ref,
                 q_ref, k_ref, v_ref, za_ref, qkvb_ref, zb_ref, ga_ref, gb_ref, ab_ref):
    xb = x_ref[...].astype(BF16)
    ct, sa, sb = ct_ref[...], sa_ref[...], sb_ref[...]

    def seg(off, width):
        return jnp.dot(xb, w_ref[:, off:off + width], preferred_element_type=F32)

    q = seg(OFF_Q, WIDTH_A)
    for j in range(WIDTH_A // LANES):
        qj = _rope_cols(q[:, j * LANES:(j + 1) * LANES], ct, sa, sb) * (HEAD_DIM_A ** -0.5)
        q_ref[:, j * LANES:(j + 1) * LANES] = qj.astype(q_ref.dtype)
    k = seg(OFF_K, KV_WIDTH_A)
    for j in range(KV_WIDTH_A // LANES):
        k_ref[:, j * LANES:(j + 1) * LANES] = _rope_cols(k[:, j * LANES:(j + 1) * LANES], ct, sa, sb)
    v_ref[...] = seg(OFF_V, KV_WIDTH_A)
    za_ref[...] = seg(OFF_ZA, WIDTH_A)
    for j in range(CONV_CH // D_MODEL):
        qkvb_ref[:, j * D_MODEL:(j + 1) * D_MODEL] = seg(OFF_QKVB + j * D_MODEL, D_MODEL)
    zb_ref[...] = seg(OFF_ZB, WIDTH_B)
    ga_ref[...] = seg(OFF_GA, D_MODEL)
    gb_ref[...] = seg(OFF_GB, D_MODEL)
    ab_ref[...] = seg(OFF_AB, AB_PAD)


def _projection(x2d, w_r, ct, sa, sb, tm):
    n = x2d.shape[0]
    t_rows = ct.shape[0]
    n_t = t_rows // tm
    row = lambda w: pl.BlockSpec((tm, w), lambda i: (i, 0))
    tab = pl.BlockSpec((tm, LANES), lambda i: (i % n_t, 0))
    widths = (WIDTH_A, KV_WIDTH_A, KV_WIDTH_A, WIDTH_A, CONV_CH, WIDTH_B, D_MODEL, D_MODEL, AB_PAD)
    dtypes = (BF16, F32, F32, F32, F32, F32, F32, F32, F32)
    return pl.pallas_call(
        _proj_kernel,
        grid=(n // tm,),
        in_specs=[row(D_MODEL),
                  pl.BlockSpec((D_MODEL, D_IN_PAD), lambda i: (0, 0), pipeline_mode=pl.Buffered(1)),
                  tab, tab, tab],
        out_specs=[row(w) for w in widths],
        out_shape=[jax.ShapeDtypeStruct((n, w), d) for w, d in zip(widths, dtypes)],
        compiler_params=pltpu.CompilerParams(dimension_semantics=("parallel",),
                                             vmem_limit_bytes=VMEM_LIMIT),
        name="in_projection",
    )(x2d, w_r, ct, sa, sb)


def _attn_kernel(sinks_ref, q_ref, kp2_ref, kp1_ref, kc_ref, vp2_ref, vp1_ref, vc_ref, za_ref, o_ref,
                 *, first_chunks_masked):
    c = pl.program_id(1)
    n_keys = 3 * CHUNK
    kext = jnp.concatenate([kp2_ref[...], kp1_ref[...], kc_ref[...]], axis=0)
    vext = jnp.concatenate([vp2_ref[...], vp1_ref[...], vc_ref[...]], axis=0)
    lo = lax.broadcasted_iota(jnp.int32, (n_keys, LANES), 1) < HEAD_DIM_A
    top = lax.broadcasted_iota(jnp.int32, (2 * CHUNK, 1), 0) < CHUNK
    if first_chunks_masked:
        key_min = jnp.where(c >= 2, 0, jnp.where(c >= 1, CHUNK, 2 * CHUNK))
        key_ok = lax.broadcasted_iota(jnp.int32, (2 * CHUNK, n_keys), 1) >= key_min
    for kv in range(N_KV_A):
        col, odd = kv // 2, kv % 2
        kc = kext[:, col * LANES:(col + 1) * LANES]
        vc = vext[:, col * LANES:(col + 1) * LANES]
        ks = pltpu.roll(kc, HEAD_DIM_A, axis=1)
        vs = pltpu.roll(vc, HEAD_DIM_A, axis=1)
        if odd:
            kc, ks, vc, vs = ks, kc, vs, vc
        k_lo = jnp.where(lo, kc, 0.0).astype(BF16)
        k_hi = jnp.where(lo, 0.0, ks).astype(BF16)
        v_lo = jnp.where(lo, vc, 0.0).astype(BF16)
        v_hi = jnp.where(lo, 0.0, vs).astype(BF16)
        base = kv * GROUP_A * HEAD_DIM_A
        qst = jnp.concatenate([q_ref[:, base:base + LANES], q_ref[:, base + LANES:base + 2 * LANES]], axis=0)
        acc = None
        for half, (k_m, v_m) in enumerate(((k_lo, v_lo), (k_hi, v_hi))):
            s = _dot_nt(qst, k_m)
            if first_chunks_masked:
                s = jnp.where(key_ok, s, NEG_INF)
            sink = jnp.where(top, sinks_ref[kv * GROUP_A + half], sinks_ref[kv * GROUP_A + 2 + half])
            m = jnp.maximum(jnp.max(s, axis=-1, keepdims=True), sink)
            p = jnp.exp(s - m)
            den = jnp.sum(p, axis=-1, keepdims=True) + jnp.exp(sink - m)
            o = _dot(p, v_m) * (1.0 / den)
            acc = o if acc is None else acc + o
        for pair in range(2):
            sl = slice(base + pair * LANES, base + (pair + 1) * LANES)
            gate = _silu(za_ref[:, sl])
            o_ref[:, sl] = (acc[pair * CHUNK:(pair + 1) * CHUNK] * gate).astype(o_ref.dtype)


def _attention(sinks, q, k, v, za, k_hist, v_hist, bsz, n_c):
    n = q.shape[0]
    cur = lambda w: pl.BlockSpec((CHUNK, w), lambda b, c: (b * n_c + c, 0))
    if k_hist is None:
        prev = lambda d: pl.BlockSpec((CHUNK, KV_WIDTH_A), lambda b, c: (b * n_c + jnp.maximum(c - d, 0), 0))
        kp2, kp1, vp2, vp1 = k, k, v, v
        p2, p1 = prev(2), prev(1)
    else:
        assert n_c == 1
        p2 = pl.BlockSpec((CHUNK, KV_WIDTH_A), lambda b, c: (2 * b, 0))
        p1 = pl.BlockSpec((CHUNK, KV_WIDTH_A), lambda b, c: (2 * b + 1, 0))
        kp2, kp1, vp2, vp1 = k_hist, k_hist, v_hist, v_hist
    return pl.pallas_call(
        functools.partial(_attn_kernel, first_chunks_masked=k_hist is None),
        grid=(bsz, n_c),
        in_specs=[pl.BlockSpec(memory_space=pltpu.SMEM),
                  cur(WIDTH_A), p2, p1, cur(KV_WIDTH_A), p2, p1, cur(KV_WIDTH_A), cur(WIDTH_A)],
        out_specs=cur(WIDTH_A),
        out_shape=jax.ShapeDtypeStruct((n, WIDTH_A), BF16),
        compiler_params=pltpu.CompilerParams(dimension_semantics=("parallel", "parallel"),
                                             vmem_limit_bytes=VMEM_LIMIT),
        name="band_attention",
    )(sinks, q, kp2, kp1, k, vp2, vp1, v, za)


HIST_ROW0 = 8 - (CONV_W - 1)


def _lane_col(x, lane_idx, j):
    return jnp.sum(jnp.where(lane_idx == j, x, 0.0), axis=-1, keepdims=True)


def _delta_kernel(*refs, has_state):
    if has_state:
        (qkvb_ref, ab_ref, zb_ref, cw_ref, alog_ref, dtb_ref, nw_ref, hist_ref, s0_ref,
         o_ref, s_ref, ext_ref) = refs
    else:
        (qkvb_ref, ab_ref, zb_ref, cw_ref, alog_ref, dtb_ref, nw_ref,
         o_ref, s_ref, ext_ref) = refs
    c = pl.program_id(1)

    @pl.when(c == 0)
    def _():
        if has_state:
            s_ref[...] = s0_ref[...]
            ext_ref[0:8, :] = hist_ref[0]
        else:
            s_ref[...] = jnp.zeros_like(s_ref)
            ext_ref[0:8, :] = jnp.zeros((8, CONV_CH), F32)

    ext_ref[8:8 + CHUNK, :] = qkvb_ref[...]
    conv = ext_ref[HIST_ROW0 + 0:HIST_ROW0 + CHUNK, :] * cw_ref[0:1, :]
    for j in range(1, CONV_W):
        conv = conv + ext_ref[HIST_ROW0 + j:HIST_ROW0 + j + CHUNK, :] * cw_ref[j:j + 1, :]
    conv = _silu(conv)
    ext_ref[0:8, :] = ext_ref[CHUNK:CHUNK + 8, :]

    ab = ab_ref[...]
    g = -jnp.exp(alog_ref[...]) * _softplus(ab + dtb_ref[...])
    beta_all = _sigmoid(ab)
    ri = lax.broadcasted_iota(jnp.int32, (CHUNK, CHUNK), 0)
    ci = lax.broadcasted_iota(jnp.int32, (CHUNK, CHUNK), 1)
    causal = ri >= ci
    strict = ri > ci
    tri = jnp.where(causal, 1.0, 0.0).astype(F32)
    gc = jnp.dot(tri, g, preferred_element_type=F32, precision=lax.Precision.HIGHEST)
    gc_t = gc.T
    gl_row = gc[CHUNK - 1:CHUNK, :]
    e_gc_all = jnp.exp(gc)
    e_rev_all = jnp.exp(gl_row - gc)
    e_gl_all = jnp.broadcast_to(jnp.exp(gl_row), (HEAD_DK_B, LANES))
    lane_idx = lax.broadcasted_iota(jnp.int32, (CHUNK, LANES), 1)
    lane_idx_s = lax.broadcasted_iota(jnp.int32, (HEAD_DK_B, LANES), 1)
    eye = jnp.where(ri == ci, 1.0, 0.0).astype(F32)

    for h in range(N_HEADS_B):
        hs = slice(h * HEAD_DK_B, (h + 1) * HEAD_DK_B)
        q = conv[:, h * HEAD_DK_B:(h + 1) * HEAD_DK_B]
        k = conv[:, QK_WIDTH_B + h * HEAD_DK_B:QK_WIDTH_B + (h + 1) * HEAD_DK_B]
        v = conv[:, 2 * QK_WIDTH_B + h * HEAD_DV_B:2 * QK_WIDTH_B + (h + 1) * HEAD_DV_B]
        q = q * lax.rsqrt(jnp.sum(q * q, axis=-1, keepdims=True) + L2_EPS) * (HEAD_DK_B ** -0.5)
        k = k * lax.rsqrt(jnp.sum(k * k, axis=-1, keepdims=True) + L2_EPS)
        gc_col = _lane_col(gc, lane_idx, h)
        beta = _lane_col(beta_all, lane_idx, N_HEADS_B + h)
        gc_row = gc_t[h:h + 1, :]
        decay = jnp.where(causal, jnp.exp(jnp.where(causal, gc_col - gc_row, 0.0)), 0.0)
        k_beta = k * beta
        m = jnp.where(strict, _dot_nt(k_beta, k) * decay, 0.0)
        t_inv = eye - m
        pw = m
        for _ in range(5):
            pw = _dot(pw, pw)
            t_inv = t_inv + _dot(t_inv, pw)
        e_gc = _lane_col(e_gc_all, lane_idx, h)
        u = _dot(t_inv, v * beta)
        w = _dot(t_inv, k_beta * e_gc)
        qk = _dot_nt(q, k) * decay
        q_dec = q * e_gc
        k_dec = k * _lane_col(e_rev_all, lane_idx, h)
        s_old = s_ref[0, h]
        v_new = u - _dot(w, s_old)
        o = _dot(q_dec, s_old) + _dot(qk, v_new)
        s_ref[0, h] = s_old * _lane_col(e_gl_all, lane_idx_s, h) + _dot(k_dec.T, v_new)
        o = o * lax.rsqrt(jnp.mean(o * o, axis=-1, keepdims=True) + RMS_EPS) * nw_ref[...]
        o_ref[:, hs] = (o * _silu(zb_ref[:, hs])).astype(o_ref.dtype)


def _delta(qkvb, ab, zb, conv_w, alog_row, dtb_row, nw_row, hist8, s0, bsz, n_c):
    n = qkvb.shape[0]
    cur = lambda w: pl.BlockSpec((CHUNK, w), lambda b, c: (b * n_c + c, 0))
    const = lambda shape: pl.BlockSpec(shape, lambda b, c: (0,) * len(shape))
    state = pl.BlockSpec((1, N_HEADS_B, HEAD_DK_B, HEAD_DV_B), lambda b, c: (b, 0, 0, 0))
    in_specs = [cur(CONV_CH), cur(AB_PAD), cur(WIDTH_B), const((CONV_W, CONV_CH)),
                const((1, AB_PAD)), const((1, AB_PAD)), const((1, HEAD_DV_B))]
    args = [qkvb, ab, zb, conv_w, alog_row, dtb_row, nw_row]
    has_state = s0 is not None
    if has_state:
        in_specs += [pl.BlockSpec((1, 8, CONV_CH), lambda b, c: (b, 0, 0)), state]
        args += [hist8, s0]
    return pl.pallas_call(
        functools.partial(_delta_kernel, has_state=has_state),
        grid=(bsz, n_c),
        in_specs=in_specs,
        out_specs=[cur(WIDTH_B), state],
        out_shape=[jax.ShapeDtypeStruct((n, WIDTH_B), BF16),
                   jax.ShapeDtypeStruct((bsz, N_HEADS_B, HEAD_DK_B, HEAD_DV_B), F32)],
        scratch_shapes=[pltpu.VMEM((8 + CHUNK, CONV_CH), F32)],
        compiler_params=pltpu.CompilerParams(dimension_semantics=("parallel", "arbitrary"),
                                             vmem_limit_bytes=VMEM_LIMIT),
        name="gated_delta",
    )(*args)


def _out_kernel(oa_ref, ob_ref, ga_ref, gb_ref, x_ref, woa_ref, wob_ref, wout_ref, lng_ref, lnb_ref, y_ref):
    y_a = jnp.dot(oa_ref[...], woa_ref[...], preferred_element_type=F32)
    y_b = jnp.dot(ob_ref[...], wob_ref[...], preferred_element_type=F32)
    hmix = _sigmoid(ga_ref[...]) * y_a + _sigmoid(gb_ref[...]) * y_b
    sub = jnp.dot(hmix.astype(BF16), wout_ref[...], preferred_element_type=F32)
    r = ALPHA * x_ref[...] + sub
    mu = jnp.mean(r, axis=-1, keepdims=True)
    d = r - mu
    var = jnp.mean(d * d, axis=-1, keepdims=True)
    y_ref[...] = d * lax.rsqrt(var + LN_EPS) * lng_ref[...] + lnb_ref[...]


def _output(oa, ob, ga, gb, x2d, woa, wob, wout, lng, lnb, tm):
    n = x2d.shape[0]
    row = pl.BlockSpec((tm, D_MODEL), lambda i: (i, 0))
    wspec = pl.BlockSpec((D_MODEL, D_MODEL), lambda i: (0, 0))
    vec = pl.BlockSpec((1, D_MODEL), lambda i: (0, 0))
    return pl.pallas_call(
        _out_kernel,
        grid=(n // tm,),
        in_specs=[row, row, row, row, row, wspec, wspec, wspec, vec, vec],
        out_specs=row,
        out_shape=jax.ShapeDtypeStruct((n, D_MODEL), F32),
        compiler_params=pltpu.CompilerParams(dimension_semantics=("parallel",),
                                             vmem_limit_bytes=VMEM_LIMIT),
        name="out_projection",
    )(oa, ob, ga, gb, x2d, woa, wob, wout, lng, lnb)


def _rope_tables(pos):
    half = ROT_DIM // 2
    inv_freq = jnp.power(ROPE_THETA, -jnp.arange(half, dtype=F32) / half)
    ang = pos[:, None] * inv_freq[None, :]
    cos, sin = jnp.cos(ang), jnp.sin(ang)
    n = pos.shape[0]
    ones = jnp.ones((n, HEAD_DIM_A - ROT_DIM), F32)
    zeros = jnp.zeros((n, HEAD_DIM_A - ROT_DIM), F32)
    zh = jnp.zeros((n, half), F32)
    ct = jnp.concatenate([cos, cos, ones], axis=1)
    sa = jnp.concatenate([zh, sin, zeros], axis=1)
    sb = jnp.concatenate([-sin, zh, zeros], axis=1)
    tile2 = lambda t: jnp.concatenate([t, t], axis=1)
    return tile2(ct), tile2(sa), tile2(sb)


def _layer(x, pos_offset, k_hist, v_hist, conv_hist, s0, weights, tm):
    (w_r, sinks, conv_w, alog_row, dtb_row, nw_row, woa, wob, wout, lng, lnb) = weights
    bsz, l_len, _ = x.shape
    n = bsz * l_len
    n_c = l_len // CHUNK
    x2d = x.reshape(n, D_MODEL)
    pos = jnp.arange(l_len, dtype=F32) + pos_offset
    ct, sa, sb = _rope_tables(pos)
    if l_len < tm:
        rep = tm // l_len
        ct, sa, sb = (jnp.tile(t, (rep, 1)) for t in (ct, sa, sb))
    q, k, v, za, qkvb, zb, ga, gb, ab = _projection(x2d, w_r, ct, sa, sb, tm)

    if k_hist is None:
        kh = vh = hist8 = None
    else:
        kh = k_hist.reshape(bsz * WINDOW, KV_WIDTH_A)
        vh = v_hist.reshape(bsz * WINDOW, KV_WIDTH_A)
        hist8 = jnp.pad(conv_hist, ((0, 0), (HIST_ROW0, 0), (0, 0)))
    oa = _attention(sinks, q, k, v, za, kh, vh, bsz, n_c)
    ob, s_new = _delta(qkvb, ab, zb, conv_w, alog_row, dtb_row, nw_row, hist8, s0, bsz, n_c)
    y = _output(oa, ob, ga, gb, x2d, woa, wob, wout, lng, lnb, tm)

    k3 = k.reshape(bsz, l_len, N_KV_A, HEAD_DIM_A)
    v3 = v.reshape(bsz, l_len, N_KV_A, HEAD_DIM_A)
    if k_hist is None:
        new_k, new_v = k3[:, l_len - WINDOW:], v3[:, l_len - WINDOW:]
    else:
        keep = k_hist.shape[1]
        new_k = jnp.concatenate([k_hist, k3], axis=1)[:, l_len:]
        new_v = jnp.concatenate([v_hist, v3], axis=1)[:, l_len:]
        assert new_k.shape[1] == keep == WINDOW
    new_conv = qkvb.reshape(bsz, l_len, CONV_CH)[:, l_len - (CONV_W - 1):]
    return y.reshape(bsz, l_len, D_MODEL), new_k, new_v, new_conv, s_new


def kernel(x_prompt, x_sample, cache_attn_k, cache_attn_v, state_conv, state_delta, w_in, attn_sinks, conv_w,
           a_log, dt_bias, delta_norm_w, w_o_attn, w_o_delta, w_out, ln_g, ln_b):
    splits = np.cumsum((WIDTH_A, KV_WIDTH_A, KV_WIDTH_A, WIDTH_A, CONV_CH, N_HEADS_B, N_HEADS_B, WIDTH_B,
                        D_MODEL, D_MODEL))
    ab_lo, ab_hi = int(splits[4]), int(splits[6])
    w_r = jnp.concatenate([w_in[:, :ab_lo], w_in[:, ab_hi:], w_in[:, ab_lo:ab_hi],
                           jnp.zeros((D_MODEL, AB_PAD - (ab_hi - ab_lo)), w_in.dtype)], axis=1).astype(BF16)
    pad_row = lambda t: jnp.pad(t.astype(F32), (0, AB_PAD - t.shape[0])).reshape(1, AB_PAD)
    weights = (w_r, attn_sinks.astype(F32), conv_w.astype(F32), pad_row(a_log), pad_row(dt_bias),
               delta_norm_w.astype(F32).reshape(1, HEAD_DV_B), w_o_attn.astype(BF16), w_o_delta.astype(BF16),
               w_out.astype(BF16), ln_g.astype(F32).reshape(1, D_MODEL), ln_b.astype(F32).reshape(1, D_MODEL))
    yp, kp, vp, cp, sp = _layer(x_prompt, 0.0, None, None, None, None, weights, tm=256)
    ys, ks, vs, cs, ss = _layer(x_sample, float(PAST_LEN), cache_attn_k, cache_attn_v, state_conv, state_delta,
                                weights, tm=256)
    return (yp, ys, kp, vp, cp, sp, ks, vs, cs, ss)
```

```python
import functools

import numpy as np
import jax
import jax.numpy as jnp
from jax import lax
from jax.experimental import pallas as pl
from jax.experimental.pallas import tpu as pltpu

D_MODEL = 1024
CHUNK = 64
N_HEADS_A = 16
N_KV_A = 4
HEAD_DIM_A = 64
GROUP_A = N_HEADS_A // N_KV_A
WINDOW = 128
ROT_DIM = HEAD_DIM_A // 4
ROPE_THETA = 500000.0
WIDTH_A = N_HEADS_A * HEAD_DIM_A
KV_WIDTH_A = N_KV_A * HEAD_DIM_A
N_HEADS_B = 8
HEAD_DK_B = 128
HEAD_DV_B = 128
CONV_W = 4
QK_WIDTH_B = N_HEADS_B * HEAD_DK_B
WIDTH_B = N_HEADS_B * HEAD_DV_B
CONV_CH = 2 * QK_WIDTH_B + WIDTH_B
DEPTH = 1
PAST_LEN = 1024
ALPHA = (2.0 * DEPTH) ** 0.25
LN_EPS = 1e-5
RMS_EPS = 1e-6
L2_EPS = 1e-6
NEG_INF = -1e30

LANES = 128
AB_PAD = LANES
OFF_Q = 0
OFF_K = OFF_Q + WIDTH_A
OFF_V = OFF_K + KV_WIDTH_A
OFF_ZA = OFF_V + KV_WIDTH_A
OFF_QKVB = OFF_ZA + WIDTH_A
OFF_ZB = OFF_QKVB + CONV_CH
OFF_GA = OFF_ZB + WIDTH_B
OFF_GB = OFF_GA + D_MODEL
OFF_AB = OFF_GB + D_MODEL
D_IN_PAD = OFF_AB + AB_PAD

VMEM_LIMIT = 56 * 1024 * 1024

F32 = jnp.float32
BF16 = jnp.bfloat16


def _sigmoid(x):
    return 1.0 / (1.0 + jnp.exp(-x))


def _silu(x):
    return x * _sigmoid(x)


def _softplus(x):
    return jnp.maximum(x, 0.0) + jnp.log1p(jnp.exp(-jnp.abs(x)))


def _dot(a, b):
    return jnp.dot(a.astype(BF16), b.astype(BF16), preferred_element_type=F32)


def _dot_nt(a, b):
    return lax.dot_general(a.astype(BF16), b.astype(BF16), (((1,), (1,)), ((), ())),
                           preferred_element_type=F32)


def _rope_cols(t, ct, sa, sb):
    return t * ct + pltpu.roll(t, 8, axis=1) * sa + pltpu.roll(t, LANES - 8, axis=1) * sb


def _proj_kernel(x_ref, w_ref, ct_ref, sa_ref, sb_ref,
                 q_ref, k_ref, v_ref, za_ref, qkvb_ref, zb_ref, ga_ref, gb_ref, ab_ref):
    xb = x_ref[...].astype(BF16)
    ct, sa, sb = ct_ref[...], sa_ref[...], sb_ref[...]

    def seg(off, width):
        return jnp.dot(xb, w_ref[:, off:off + width], preferred_element_type=F32)

    q = seg(OFF_Q, WIDTH_A)
    for j in range(WIDTH_A // LANES):
        qj = _rope_cols(q[:, j * LANES:(j + 1) * LANES], ct, sa, sb) * (HEAD_DIM_A ** -0.5)
        q_ref[:, j * LANES:(j + 1) * LANES] = qj.astype(q_ref.dtype)
    k = seg(OFF_K, KV_WIDTH_A)
    for j in range(KV_WIDTH_A // LANES):
        k_ref[:, j * LANES:(j + 1) * LANES] = _rope_cols(k[:, j * LANES:(j + 1) * LANES], ct, sa, sb)
    v_ref[...] = seg(OFF_V, KV_WIDTH_A)
    za_ref[...] = seg(OFF_ZA, WIDTH_A)
    for j in range(CONV_CH // D_MODEL):
        qkvb_ref[:, j * D_MODEL:(j + 1) * D_MODEL] = seg(OFF_QKVB + j * D_MODEL, D_MODEL)
    zb_ref[...] = seg(OFF_ZB, WIDTH_B)
    ga_ref[...] = seg(OFF_GA, D_MODEL)
    gb_ref[...] = seg(OFF_GB, D_MODEL)
    ab_ref[...] = seg(OFF_AB, AB_PAD)


def _projection(x2d, w_r, ct, sa, sb, tm):
    n = x2d.shape[0]
    t_rows = ct.shape[0]
    n_t = t_rows // tm
    row = lambda w: pl.BlockSpec((tm, w), lambda i: (i, 0))
    tab = pl.BlockSpec((tm, LANES), lambda i: (i % n_t, 0))
    widths = (WIDTH_A, KV_WIDTH_A, KV_WIDTH_A, WIDTH_A, CONV_CH, WIDTH_B, D_MODEL, D_MODEL, AB_PAD)
    dtypes = (BF16, F32, F32, F32, F32, F32, F32, F32, F32)
    return pl.pallas_call(
        _proj_kernel,
        grid=(n // tm,),
        in_specs=[row(D_MODEL),
                  pl.BlockSpec((D_MODEL, D_IN_PAD), lambda i: (0, 0), pipeline_mode=pl.Buffered(1)),
                  tab, tab, tab],
        out_specs=[row(w) for w in widths],
        out_shape=[jax.ShapeDtypeStruct((n, w), d) for w, d in zip(widths, dtypes)],
        compiler_params=pltpu.CompilerParams(dimension_semantics=("parallel",),
                                             vmem_limit_bytes=VMEM_LIMIT),
        name="in_projection",
    )(x2d, w_r, ct, sa, sb)


def _attn_kernel(sinks_ref, q_ref, kp2_ref, kp1_ref, kc_ref, vp2_ref, vp1_ref, vc_ref, za_ref, o_ref,
                 *, first_chunks_masked):
    c = pl.program_id(1)
    n_keys = 3 * CHUNK
    kext = jnp.concatenate([kp2_ref[...], kp1_ref[...], kc_ref[...]], axis=0)
    vext = jnp.concatenate([vp2_ref[...], vp1_ref[...], vc_ref[...]], axis=0)
    lo = lax.broadcasted_iota(jnp.int32, (n_keys, LANES), 1) < HEAD_DIM_A
    top = lax.broadcasted_iota(jnp.int32, (2 * CHUNK, 1), 0) < CHUNK
    if first_chunks_masked:
        key_min = jnp.where(c >= 2, 0, jnp.where(c >= 1, CHUNK, 2 * CHUNK))
        key_ok = lax.broadcasted_iota(jnp.int32, (2 * CHUNK, n_keys), 1) >= key_min
    for kv in range(N_KV_A):
        col, odd = kv // 2, kv % 2
        kc = kext[:, col * LANES:(col + 1) * LANES]
        vc = vext[:, col * LANES:(col + 1) * LANES]
        ks = pltpu.roll(kc, HEAD_DIM_A, axis=1)
        vs = pltpu.roll(vc, HEAD_DIM_A, axis=1)
        if odd:
            kc, ks, vc, vs = ks, kc, vs, vc
        k_lo = jnp.where(lo, kc, 0.0).astype(BF16)
        k_hi = jnp.where(lo, 0.0, ks).astype(BF16)
        v_lo = jnp.where(lo, vc, 0.0).astype(BF16)
        v_hi = jnp.where(lo, 0.0, vs).astype(BF16)
        base = kv * GROUP_A * HEAD_DIM_A
        qst = jnp.concatenate([q_ref[:, base:base + LANES], q_ref[:, base + LANES:base + 2 * LANES]], axis=0)
        acc = None
        for half, (k_m, v_m) in enumerate(((k_lo, v_lo), (k_hi, v_hi))):
            s = _dot_nt(qst, k_m)
            if first_chunks_masked:
                s = jnp.where(key_ok, s, NEG_INF)
            sink = jnp.where(top, sinks_ref[kv * GROUP_A + half], sinks_ref[kv * GROUP_A + 2 + half])
            m = jnp.maximum(jnp.max(s, axis=-1, keepdims=True), sink)
            p = jnp.exp(s - m)
            den = jnp.sum(p, axis=-1, keepdims=True) + jnp.exp(sink - m)
            o = _dot(p, v_m) * (1.0 / den)
            acc = o if acc is None else acc + o
        for pair in range(2):
            sl = slice(base + pair * LANES, base + (pair + 1) * LANES)
            gate = _silu(za_ref[:, sl])
            o_ref[:, sl] = (acc[pair * CHUNK:(pair + 1) * CHUNK] * gate).astype(o_ref.dtype)


def _attention(sinks, q, k, v, za, k_hist, v_hist, bsz, n_c):
    n = q.shape[0]
    cur = lambda w: pl.BlockSpec((CHUNK, w), lambda b, c: (b * n_c + c, 0))
    if k_hist is None:
        prev = lambda d: pl.BlockSpec((CHUNK, KV_WIDTH_A), lambda b, c: (b * n_c + jnp.maximum(c - d, 0), 0))
        kp2, kp1, vp2, vp1 = k, k, v, v
        p2, p1 = prev(2), prev(1)
    else:
        assert n_c == 1
        p2 = pl.BlockSpec((CHUNK, KV_WIDTH_A), lambda b, c: (2 * b, 0))
        p1 = pl.BlockSpec((CHUNK, KV_WIDTH_A), lambda b, c: (2 * b + 1, 0))
        kp2, kp1, vp2, vp1 = k_hist, k_hist, v_hist, v_hist
    return pl.pallas_call(
        functools.partial(_attn_kernel, first_chunks_masked=k_hist is None),
        grid=(bsz, n_c),
        in_specs=[pl.BlockSpec(memory_space=pltpu.SMEM),
                  cur(WIDTH_A), p2, p1, cur(KV_WIDTH_A), p2, p1, cur(KV_WIDTH_A), cur(WIDTH_A)],
        out_specs=cur(WIDTH_A),
        out_shape=jax.ShapeDtypeStruct((n, WIDTH_A), BF16),
        compiler_params=pltpu.CompilerParams(dimension_semantics=("parallel", "parallel"),
                                             vmem_limit_bytes=VMEM_LIMIT),
        name="band_attention",
    )(sinks, q, kp2, kp1, k, vp2, vp1, v, za)


HIST_ROW0 = 8 - (CONV_W - 1)
DELTA_CHUNKS_PER_STEP = 4


def _interleave(streams):
    tagged = [((i + 0.5) / len(s), si, i, t) for si, s in enumerate(streams) for i, t in enumerate(s)]
    tagged.sort(key=lambda x: x[:3])
    return [t for *_, t in tagged]


def _lane_col(x, lane_idx, j):
    return jnp.sum(jnp.where(lane_idx == j, x, 0.0), axis=-1, keepdims=True)


def _delta_kernel(*refs, has_state, n_ch):
    if has_state:
        (qkvb_ref, ab_ref, zb_ref, cw_ref, alog_ref, dtb_ref, nw_ref, hist_ref, s0_ref,
         o_ref, s_ref, ext_ref) = refs
    else:
        (qkvb_ref, ab_ref, zb_ref, cw_ref, alog_ref, dtb_ref, nw_ref,
         o_ref, s_ref, ext_ref) = refs
    c = pl.program_id(1)

    @pl.when(c == 0)
    def _():
        if has_state:
            s_ref[...] = s0_ref[...]
            ext_ref[0:8, :] = hist_ref[0]
        else:
            s_ref[...] = jnp.zeros_like(s_ref)
            ext_ref[0:8, :] = jnp.zeros((8, CONV_CH), F32)

    ext_ref[8:8 + n_ch * CHUNK, :] = qkvb_ref[...]
    ri = lax.broadcasted_iota(jnp.int32, (CHUNK, CHUNK), 0)
    ci = lax.broadcasted_iota(jnp.int32, (CHUNK, CHUNK), 1)
    causal = ri >= ci
    strict = ri > ci
    eye = jnp.where(ri == ci, 1.0, 0.0).astype(F32)
    lane_idx = lax.broadcasted_iota(jnp.int32, (CHUNK, LANES), 1)
    units = [(c, h) for c in range(n_ch) for h in range(N_HEADS_B)]

    def conv_silu(c, col):
        cols = slice(col * LANES, (col + 1) * LANES)
        base = 8 + c * CHUNK
        xp = ext_ref[base - 8:base + CHUNK, cols]
        acc = xp[8:] * cw_ref[CONV_W - 1:CONV_W, cols]
        for j in range(1, CONV_W):
            acc = acc + pltpu.roll(xp, j, axis=0)[8:] * cw_ref[CONV_W - 1 - j:CONV_W - j, cols]
        return _silu(acc)

    heads = range(N_HEADS_B)
    ops = {(c, h): {} for c, h in units}
    gates = [None] * n_ch
    triu = jnp.where(ri <= ci, 1.0, 0.0).astype(F32)
    sub_idx = lax.broadcasted_iota(jnp.int32, (N_HEADS_B, CHUNK), 1)

    def gate_stage(c):
        ab_t = ab_ref[c * CHUNK:(c + 1) * CHUNK, :].T
        g_t = -jnp.exp(alog_ref[...]) * _softplus(ab_t[0:N_HEADS_B] + dtb_ref[...])
        gc_t = jnp.dot(g_t, triu, preferred_element_type=F32, precision=lax.Precision.HIGHEST)
        gl = jnp.sum(jnp.where(sub_idx == CHUNK - 1, gc_t, 0.0), axis=-1, keepdims=True)
        pad = jnp.zeros((LANES - 5 * N_HEADS_B, CHUNK), F32)
        packed = jnp.concatenate([gc_t, jnp.exp(gc_t), jnp.exp(gl - gc_t), -_sigmoid(ab_t[N_HEADS_B:2 * N_HEADS_B]),
                                  jnp.broadcast_to(jnp.exp(gl), (N_HEADS_B, CHUNK)), pad], axis=0)
        gates[c] = dict(gc_t=gc_t, cols=packed.T)

    def prep(c, h):
        gt, op = gates[c], ops[c, h]
        col = lambda i: _lane_col(gt["cols"], lane_idx, N_HEADS_B * i + h)
        q = conv_silu(c, h)
        k = conv_silu(c, N_HEADS_B + h)
        v = conv_silu(c, 2 * N_HEADS_B + h)
        q = q * (lax.rsqrt(jnp.sum(q * q, axis=-1, keepdims=True) + L2_EPS) * (HEAD_DK_B ** -0.5))
        k = k * lax.rsqrt(jnp.sum(k * k, axis=-1, keepdims=True) + L2_EPS)
        gc_col, e_gc, e_rev, nb, e_gl = col(0), col(1), col(2), col(3), col(4)
        gc_row = gt["gc_t"][h:h + 1, :]
        kbn = k * nb
        op.update(
            lhs_a=jnp.concatenate([kbn.astype(BF16), q.astype(BF16)], axis=0),
            k=k.astype(BF16),
            rhs_uw=jnp.concatenate([(v * -nb).astype(BF16), (kbn * e_gc).astype(BF16)], axis=1),
            q_dec=(q * e_gc).astype(BF16),
            k_dec_t=(k * e_rev).T.astype(BF16),
            decay=jnp.where(causal, jnp.exp(jnp.where(causal, gc_col - gc_row, 0.0)), 0.0),
            e_gl=jnp.concatenate([e_gl, e_gl], axis=0))

    def scores(c, h):
        op = ops[c, h]
        aq = _dot_nt(op["lhs_a"], op["k"])
        op["p"] = jnp.where(strict, aq[:CHUNK] * op["decay"], 0.0)
        op["qk"] = (aq[CHUNK:] * op["decay"]).astype(BF16)

    def inv_first(c, h):
        op = ops[c, h]
        op["t"] = eye + op["p"]
        pb = op["p"].astype(BF16)
        op["p"] = jnp.dot(pb, pb, preferred_element_type=F32)

    def inv_mid(c, h):
        op = ops[c, h]
        pb = op["p"].astype(BF16)
        r = jnp.dot(jnp.concatenate([pb, op["t"].astype(BF16)], axis=0), pb, preferred_element_type=F32)
        op["p"] = r[:CHUNK]
        op["t"] = op["t"] + r[CHUNK:]

    def inv_last(c, h):
        op = ops[c, h]
        op["t"] = op["t"] + _dot(op["t"], op["p"])

    def apply_inv(c, h):
        op = ops[c, h]
        uw = _dot(op["t"], op["rhs_uw"])
        op["u"] = uw[:, :HEAD_DV_B]
        op["wq"] = jnp.concatenate([uw[:, HEAD_DV_B:].astype(BF16), op["q_dec"]], axis=0)

    state = [None] * N_HEADS_B

    def load_state(h):
        state[h] = s_ref[0, h]

    def rec_first(c, h):
        op = ops[c, h]
        ws = _dot(op["wq"], state[h])
        op["v_new"] = (op["u"] + ws[:CHUNK]).astype(BF16)
        op["qs"] = ws[CHUNK:]

    def rec_second(c, h):
        op = ops[c, h]
        rows = slice(c * CHUNK, (c + 1) * CHUNK)
        hs = slice(h * HEAD_DV_B, (h + 1) * HEAD_DV_B)
        r = jnp.dot(jnp.concatenate([op["qk"], op["k_dec_t"]], axis=0), op["v_new"],
                    preferred_element_type=F32)
        o = op["qs"] + r[:CHUNK]
        state[h] = state[h] * op["e_gl"] + r[CHUNK:]
        o = o * lax.rsqrt(jnp.mean(o * o, axis=-1, keepdims=True) + RMS_EPS) * nw_ref[...]
        o_ref[rows, hs] = (o * _silu(zb_ref[rows, hs])).astype(o_ref.dtype)
        ops[c, h] = None

    def per_head(fn, c):
        return [functools.partial(fn, c, h) for h in heads]

    def prep_stream(c):
        return [functools.partial(gate_stage, c)] + per_head(prep, c)

    def inverse_stream(c):
        levels = [scores, inv_first] + [inv_mid] * 4 + [inv_last, apply_inv]
        return [t for fn in levels for t in per_head(fn, c)]

    def recurrence_stream(c):
        return per_head(rec_first, c) + per_head(rec_second, c)

    for h in heads:
        load_state(h)
    for step in range(n_ch + 2):
        streams = []
        if step - 2 >= 0:
            streams.append(recurrence_stream(step - 2))
        if 0 <= step - 1 < n_ch:
            streams.append(inverse_stream(step - 1))
        if step < n_ch:
            streams.append(prep_stream(step))
        for thunk in _interleave(streams):
            thunk()
    ext_ref[0:8, :] = ext_ref[n_ch * CHUNK:n_ch * CHUNK + 8, :]
    for h in heads:
        s_ref[0, h] = state[h]


def _delta(qkvb, ab, zb, conv_w, alog_row, dtb_row, nw_row, hist8, s0, bsz, n_c, n_ch):
    n = qkvb.shape[0]
    assert n_c % n_ch == 0
    n_c = n_c // n_ch
    cur = lambda w: pl.BlockSpec((n_ch * CHUNK, w), lambda b, c: (b * n_c + c, 0))
    const = lambda shape: pl.BlockSpec(shape, lambda b, c: (0,) * len(shape))
    state = pl.BlockSpec((1, N_HEADS_B, HEAD_DK_B, HEAD_DV_B), lambda b, c: (b, 0, 0, 0))
    in_specs = [cur(CONV_CH), cur(AB_PAD), cur(WIDTH_B), const((CONV_W, CONV_CH)),
                const((N_HEADS_B, 1)), const((N_HEADS_B, 1)), const((1, HEAD_DV_B))]
    args = [qkvb, ab, zb, conv_w, alog_row, dtb_row, nw_row]
    has_state = s0 is not None
    if has_state:
        in_specs += [pl.BlockSpec((1, 8, CONV_CH), lambda b, c: (b, 0, 0)), state]
        args += [hist8, s0]
    return pl.pallas_call(
        functools.partial(_delta_kernel, has_state=has_state, n_ch=n_ch),
        grid=(bsz, n_c),
        in_specs=in_specs,
        out_specs=[cur(WIDTH_B), state],
        out_shape=[jax.ShapeDtypeStruct((n, WIDTH_B), BF16),
                   jax.ShapeDtypeStruct((bsz, N_HEADS_B, HEAD_DK_B, HEAD_DV_B), F32)],
        scratch_shapes=[pltpu.VMEM((8 + n_ch * CHUNK, CONV_CH), F32)],
        compiler_params=pltpu.CompilerParams(dimension_semantics=("parallel", "arbitrary"),
                                             vmem_limit_bytes=VMEM_LIMIT),
        name="gated_delta",
    )(*args)


def _out_kernel(oa_ref, ob_ref, ga_ref, gb_ref, x_ref, woa_ref, wob_ref, wout_ref, lng_ref, lnb_ref, y_ref):
    y_a = jnp.dot(oa_ref[...], woa_ref[...], preferred_element_type=F32)
    y_b = jnp.dot(ob_ref[...], wob_ref[...], preferred_element_type=F32)
    hmix = _sigmoid(ga_ref[...]) * y_a + _sigmoid(gb_ref[...]) * y_b
    sub = jnp.dot(hmix.astype(BF16), wout_ref[...], preferred_element_type=F32)
    r = ALPHA * x_ref[...] + sub
    mu = jnp.mean(r, axis=-1, keepdims=True)
    d = r - mu
    var = jnp.mean(d * d, axis=-1, keepdims=True)
    y_ref[...] = d * lax.rsqrt(var + LN_EPS) * lng_ref[...] + lnb_ref[...]


def _output(oa, ob, ga, gb, x2d, woa, wob, wout, lng, lnb, tm):
    n = x2d.shape[0]
    row = pl.BlockSpec((tm, D_MODEL), lambda i: (i, 0))
    wspec = pl.BlockSpec((D_MODEL, D_MODEL), lambda i: (0, 0))
    vec = pl.BlockSpec((1, D_MODEL), lambda i: (0, 0))
    return pl.pallas_call(
        _out_kernel,
        grid=(n // tm,),
        in_specs=[row, row, row, row, row, wspec, wspec, wspec, vec, vec],
        out_specs=row,
        out_shape=jax.ShapeDtypeStruct((n, D_MODEL), F32),
        compiler_params=pltpu.CompilerParams(dimension_semantics=("parallel",),
                                             vmem_limit_bytes=VMEM_LIMIT),
        name="out_projection",
    )(oa, ob, ga, gb, x2d, woa, wob, wout, lng, lnb)


def _rope_tables(pos):
    half = ROT_DIM // 2
    inv_freq = jnp.power(ROPE_THETA, -jnp.arange(half, dtype=F32) / half)
    ang = pos[:, None] * inv_freq[None, :]
    cos, sin = jnp.cos(ang), jnp.sin(ang)
    n = pos.shape[0]
    ones = jnp.ones((n, HEAD_DIM_A - ROT_DIM), F32)
    zeros = jnp.zeros((n, HEAD_DIM_A - ROT_DIM), F32)
    zh = jnp.zeros((n, half), F32)
    ct = jnp.concatenate([cos, cos, ones], axis=1)
    sa = jnp.concatenate([zh, sin, zeros], axis=1)
    sb = jnp.concatenate([-sin, zh, zeros], axis=1)
    tile2 = lambda t: jnp.concatenate([t, t], axis=1)
    return tile2(ct), tile2(sa), tile2(sb)


def _layer(x, pos_offset, k_hist, v_hist, conv_hist, s0, weights, tm):
    (w_r, sinks, conv_w, alog_row, dtb_row, nw_row, woa, wob, wout, lng, lnb) = weights
    bsz, l_len, _ = x.shape
    n = bsz * l_len
    n_c = l_len // CHUNK
    x2d = x.reshape(n, D_MODEL)
    pos = jnp.arange(l_len, dtype=F32) + pos_offset
    ct, sa, sb = _rope_tables(pos)
    if l_len < tm:
        rep = tm // l_len
        ct, sa, sb = (jnp.tile(t, (rep, 1)) for t in (ct, sa, sb))
    q, k, v, za, qkvb, zb, ga, gb, ab = _projection(x2d, w_r, ct, sa, sb, tm)

    if k_hist is None:
        kh = vh = hist8 = None
    else:
        kh = k_hist.reshape(bsz * WINDOW, KV_WIDTH_A)
        vh = v_hist.reshape(bsz * WINDOW, KV_WIDTH_A)
        hist8 = jnp.pad(conv_hist, ((0, 0), (HIST_ROW0, 0), (0, 0)))
    oa = _attention(sinks, q, k, v, za, kh, vh, bsz, n_c)
    ob, s_new = _delta(qkvb, ab, zb, conv_w, alog_row, dtb_row, nw_row, hist8, s0, bsz, n_c,
                      n_ch=min(n_c, DELTA_CHUNKS_PER_STEP))
    y = _output(oa, ob, ga, gb, x2d, woa, wob, wout, lng, lnb, tm)

    k3 = k.reshape(bsz, l_len, N_KV_A, HEAD_DIM_A)
    v3 = v.reshape(bsz, l_len, N_KV_A, HEAD_DIM_A)
    if k_hist is None:
        new_k, new_v = k3[:, l_len - WINDOW:], v3[:, l_len - WINDOW:]
    else:
        keep = k_hist.shape[1]
        new_k = jnp.concatenate([k_hist, k3], axis=1)[:, l_len:]
        new_v = jnp.concatenate([v_hist, v3], axis=1)[:, l_len:]
        assert new_k.shape[1] == keep == WINDOW
    new_conv = qkvb.reshape(bsz, l_len, CONV_CH)[:, l_len - (CONV_W - 1):]
    return y.reshape(bsz, l_len, D_MODEL), new_k, new_v, new_conv, s_new


def kernel(x_prompt, x_sample, cache_attn_k, cache_attn_v, state_conv, state_delta, w_in, attn_sinks, conv_w,
           a_log, dt_bias, delta_norm_w, w_o_attn, w_o_delta, w_out, ln_g, ln_b):
    splits = np.cumsum((WIDTH_A, KV_WIDTH_A, KV_WIDTH_A, WIDTH_A, CONV_CH, N_HEADS_B, N_HEADS_B, WIDTH_B,
                        D_MODEL, D_MODEL))
    ab_lo, ab_hi = int(splits[4]), int(splits[6])
    w_r = jnp.concatenate([w_in[:, :ab_lo], w_in[:, ab_hi:], w_in[:, ab_lo:ab_hi],
                           jnp.zeros((D_MODEL, AB_PAD - (ab_hi - ab_lo)), w_in.dtype)], axis=1).astype(BF16)
    head_col = lambda t: t.astype(F32).reshape(N_HEADS_B, 1)
    weights = (w_r, attn_sinks.astype(F32), conv_w.astype(F32), head_col(a_log), head_col(dt_bias),
               delta_norm_w.astype(F32).reshape(1, HEAD_DV_B), w_o_attn.astype(BF16), w_o_delta.astype(BF16),
               w_out.astype(BF16), ln_g.astype(F32).reshape(1, D_MODEL), ln_b.astype(F32).reshape(1, D_MODEL))
    yp, kp, vp, cp, sp = _layer(x_prompt, 0.0, None, None, None, None, weights, tm=256)
    ys, ks, vs, cs, ss = _layer(x_sample, float(PAST_LEN), cache_attn_k, cache_attn_v, state_conv, state_delta,
                                weights, tm=256)
    return (yp, ys, kp, vp, cp, sp, ks, vs, cs, ss)
```

```python
import functools
import math

import numpy as np
import jax
import jax.numpy as jnp
from jax import lax
from jax.experimental import pallas as pl
from jax.experimental.pallas import tpu as pltpu

D_MODEL = 1024
CHUNK = 64
N_HEADS_A = 16
N_KV_A = 4
HEAD_DIM_A = 64
GROUP_A = N_HEADS_A // N_KV_A
WINDOW = 128
ROT_DIM = HEAD_DIM_A // 4
ROPE_THETA = 500000.0
WIDTH_A = N_HEADS_A * HEAD_DIM_A
KV_WIDTH_A = N_KV_A * HEAD_DIM_A
N_HEADS_B = 8
HEAD_DK_B = 128
HEAD_DV_B = 128
CONV_W = 4
QK_WIDTH_B = N_HEADS_B * HEAD_DK_B
WIDTH_B = N_HEADS_B * HEAD_DV_B
CONV_CH = 2 * QK_WIDTH_B + WIDTH_B
DEPTH = 1
PAST_LEN = 1024
ALPHA = (2.0 * DEPTH) ** 0.25
LN_EPS = 1e-5
RMS_EPS = 1e-6
L2_EPS = 1e-6
NEG_INF = -1e30
LOG2E = math.log2(math.e)

LANES = 128
SUBLANES = 8
AB_PAD = LANES
OFF_Q = 0
OFF_K = OFF_Q + WIDTH_A
OFF_V = OFF_K + KV_WIDTH_A
OFF_ZA = OFF_V + KV_WIDTH_A
OFF_QKVB = OFF_ZA + WIDTH_A
OFF_ZB = OFF_QKVB + CONV_CH
OFF_GA = OFF_ZB + WIDTH_B
OFF_GB = OFF_GA + D_MODEL
OFF_AB = OFF_GB + D_MODEL
D_IN_PAD = OFF_AB + AB_PAD

VMEM_LIMIT = 56 * 1024 * 1024
PROJ_SEG = 512
PROMPT_TILE = 256
OUT_TILE = 512

F32 = jnp.float32
BF16 = jnp.bfloat16


def _sigmoid(x):
    return 1.0 / (1.0 + jnp.exp(-x))


def _silu(x):
    return x * _sigmoid(x)


def _softplus(x):
    return jnp.maximum(x, 0.0) + jnp.log1p(jnp.exp(-jnp.abs(x)))


def _dot(a, b):
    return jnp.dot(a.astype(BF16), b.astype(BF16), preferred_element_type=F32)


def _dot_nt(a, b):
    return lax.dot_general(a.astype(BF16), b.astype(BF16), (((1,), (1,)), ((), ())),
                           preferred_element_type=F32)


def _interleave(streams):
    tagged = [((i + 0.5) / len(s), si, i, t) for si, s in enumerate(streams) for i, t in enumerate(s)]
    tagged.sort(key=lambda x: x[:3])
    return [t for *_, t in tagged]


def _lagged(first, second, lag):
    order = []
    for i in range(len(first) + lag):
        if i < len(first):
            order.append(first[i])
        if 0 <= i - lag < len(second):
            order.append(second[i - lag])
    return order


def _rope_cols(t, ct, sa, sb):
    return t * ct + pltpu.roll(t, 8, axis=1) * sa + pltpu.roll(t, LANES - 8, axis=1) * sb


def _proj_kernel(*refs, tm, tiles_per_seq, has_hist):
    if has_hist:
        (x_ref, hist_ref, w_ref, ct_ref, sa_ref, sb_ref, cw_ref,
         q_ref, k_ref, v_ref, gza_ref, cs_ref, gzb_ref, sga_ref, sgb_ref, ab_ref, tail_ref) = refs
        xb = x_ref[...].astype(BF16)
        xb_look = xb
    else:
        (x_ref, xprev_ref, w_ref, ct_ref, sa_ref, sb_ref, cw_ref,
         q_ref, k_ref, v_ref, gza_ref, cs_ref, gzb_ref, sga_ref, sgb_ref, ab_ref, tail_ref) = refs
        xb = x_ref[...].astype(BF16)
        xb_look = jnp.concatenate([xb, xprev_ref[...].astype(BF16)], axis=0)
        first_tile = pl.program_id(0) % tiles_per_seq == 0
    ct, sa, sb = ct_ref[...], sa_ref[...], sb_ref[...]
    res = {}

    def matmul(key, off, width, lhs):
        res[key] = jnp.dot(lhs, w_ref[:, off:off + width], preferred_element_type=F32)

    def slabs(width):
        return [slice(j * LANES, (j + 1) * LANES) for j in range(width // LANES)]

    def epi_q(key, off):
        r = res.pop(key)
        for sl in slabs(r.shape[1]):
            qj = _rope_cols(r[:, sl], ct, sa, sb) * (HEAD_DIM_A ** -0.5 * LOG2E)
            q_ref[:, off + sl.start:off + sl.stop] = qj.astype(q_ref.dtype)

    def epi_kv(key):
        r = res.pop(key)
        for sl in slabs(KV_WIDTH_A):
            k_ref[:, sl] = _rope_cols(r[:, sl], ct, sa, sb)
        v_ref[...] = r[:, KV_WIDTH_A:]

    def epi_act(key, fn, out_ref, off):
        r = res.pop(key)
        out_ref[:, off:off + r.shape[1]] = fn(r).astype(out_ref.dtype)

    def epi_conv(key, off):
        r = res.pop(key)
        tail_ref[:, off:off + r.shape[1]] = r[tm - SUBLANES:tm]
        for sl in slabs(r.shape[1]):
            cols = slice(off + sl.start, off + sl.stop)
            if has_hist:
                look = hist_ref[0, :, cols]
            else:
                look = jnp.where(first_tile, 0.0, r[tm:tm + SUBLANES, sl])
            cur = r[0:tm, sl]
            xp = jnp.concatenate([look, cur], axis=0)
            acc = cur * cw_ref[CONV_W - 1:CONV_W, cols]
            for j in range(1, CONV_W):
                acc = acc + pltpu.roll(xp, j, axis=0)[SUBLANES:] * cw_ref[CONV_W - 1 - j:CONV_W - j, cols]
            cs_ref[:, cols] = _silu(acc).astype(cs_ref.dtype)

    def epi_ab(key):
        ab_ref[...] = res.pop(key)

    mm, epi = [], []

    def add(key, off, width, epilogue, lhs=None):
        mm.append(functools.partial(matmul, key, off, width, xb if lhs is None else lhs))
        epi.append(epilogue)

    for j in range(WIDTH_A // PROJ_SEG):
        add(("q", j), OFF_Q + j * PROJ_SEG, PROJ_SEG, functools.partial(epi_q, ("q", j), j * PROJ_SEG))
    add("kv", OFF_K, 2 * KV_WIDTH_A, functools.partial(epi_kv, "kv"))
    for j in range(WIDTH_A // PROJ_SEG):
        add(("za", j), OFF_ZA + j * PROJ_SEG, PROJ_SEG,
            functools.partial(epi_act, ("za", j), _silu, gza_ref, j * PROJ_SEG))
    for j in range(CONV_CH // PROJ_SEG):
        add(("qkvb", j), OFF_QKVB + j * PROJ_SEG, PROJ_SEG,
            functools.partial(epi_conv, ("qkvb", j), j * PROJ_SEG), lhs=xb_look)
    for name, base, fn, out_ref in (("zb", OFF_ZB, _silu, gzb_ref), ("ga", OFF_GA, _sigmoid, sga_ref),
                                    ("gb", OFF_GB, _sigmoid, sgb_ref)):
        for j in range(D_MODEL // PROJ_SEG):
            add((name, j), base + j * PROJ_SEG, PROJ_SEG,
                functools.partial(epi_act, (name, j), fn, out_ref, j * PROJ_SEG))
    add("ab", OFF_AB, AB_PAD, functools.partial(epi_ab, "ab"))
    for thunk in _lagged(mm, epi, 1):
        thunk()


def _projection(x2d, hist8, w_r, ct, sa, sb, conv_w, tm, tiles_per_seq):
    n = x2d.shape[0]
    n_tiles = n // tm
    t_rows = ct.shape[0]
    n_t = t_rows // tm
    has_hist = hist8 is not None
    row = lambda w: pl.BlockSpec((tm, w), lambda i: (i, 0))
    tab = pl.BlockSpec((tm, LANES), lambda i: (i % n_t, 0))
    if has_hist:
        assert tiles_per_seq == 1
        look_spec = pl.BlockSpec((1, SUBLANES, CONV_CH), lambda i: (i, 0, 0))
        look = hist8
    else:
        blocks = tm // SUBLANES
        look_spec = pl.BlockSpec((SUBLANES, D_MODEL), lambda i: (jnp.maximum(i * blocks - 1, 0), 0))
        look = x2d
    widths = (WIDTH_A, KV_WIDTH_A, KV_WIDTH_A, WIDTH_A, CONV_CH, WIDTH_B, D_MODEL, D_MODEL, AB_PAD)
    dtypes = (BF16, F32, F32, BF16, BF16, BF16, BF16, BF16, F32)
    return pl.pallas_call(
        functools.partial(_proj_kernel, tm=tm, tiles_per_seq=tiles_per_seq, has_hist=has_hist),
        grid=(n_tiles,),
        in_specs=[row(D_MODEL), look_spec,
                  pl.BlockSpec((D_MODEL, D_IN_PAD), lambda i: (0, 0), pipeline_mode=pl.Buffered(1)),
                  tab, tab, tab, pl.BlockSpec((CONV_W, CONV_CH), lambda i: (0, 0))],
        out_specs=[row(w) for w in widths] + [pl.BlockSpec((SUBLANES, CONV_CH), lambda i: (i, 0))],
        out_shape=[jax.ShapeDtypeStruct((n, w), d) for w, d in zip(widths, dtypes)]
        + [jax.ShapeDtypeStruct((n_tiles * SUBLANES, CONV_CH), F32)],
        compiler_params=pltpu.CompilerParams(dimension_semantics=("parallel",),
                                             vmem_limit_bytes=VMEM_LIMIT),
        name="in_projection",
    )(x2d, look, w_r, ct, sa, sb, conv_w)


def _attn_body(sinks_ref, q_ref, kprev_ref, kcur_ref, vprev_ref, vcur_ref, gate_ref, o_ref,
               *, n_ch, prev_valid):
    gs = 2 if n_ch % 2 == 0 else 1
    n_keys = (gs + 2) * CHUNK
    rows_u = gs * 2 * CHUNK
    kext = jnp.concatenate([kprev_ref[...], kcur_ref[...]], axis=0)
    vext = jnp.concatenate([vprev_ref[...], vcur_ref[...]], axis=0)
    lo = lax.broadcasted_iota(jnp.int32, (kext.shape[0], LANES), 1) < HEAD_DIM_A
    r_idx = lax.broadcasted_iota(jnp.int32, (rows_u, n_keys), 0)
    k_idx = lax.broadcasted_iota(jnp.int32, (rows_u, n_keys), 1)
    first_pair = (lax.broadcasted_iota(jnp.int32, (rows_u, 1), 0) // CHUNK) % 2 == 0
    ci = r_idx // (2 * CHUNK)
    band = (k_idx >= ci * CHUNK) & (k_idx < (ci + 3) * CHUNK)

    kv_ops = []
    for kv in range(N_KV_A):
        col, odd = kv // 2, kv % 2
        kc = kext[:, col * LANES:(col + 1) * LANES]
        vc = vext[:, col * LANES:(col + 1) * LANES]
        ks = pltpu.roll(kc, HEAD_DIM_A, axis=1)
        vs = pltpu.roll(vc, HEAD_DIM_A, axis=1)
        if odd:
            kc, ks, vc, vs = ks, kc, vs, vc
        kv_ops.append(((jnp.where(lo, kc, 0.0).astype(BF16), jnp.where(lo, vc, 0.0).astype(BF16)),
                       (jnp.where(lo, 0.0, ks).astype(BF16), jnp.where(lo, 0.0, vs).astype(BF16))))

    units = [(g, kv, half) for g in range(n_ch // gs) for kv in range(N_KV_A) for half in range(2)]
    st = {}

    def scores(g, kv, half):
        base = kv * GROUP_A * HEAD_DIM_A
        start = g * gs * CHUNK
        k_m = kv_ops[kv][half][0][start:start + n_keys]
        qst = jnp.concatenate(
            [q_ref[(g * gs + c) * CHUNK:(g * gs + c + 1) * CHUNK, base + pair * LANES:base + (pair + 1) * LANES]
             for c in range(gs) for pair in range(2)], axis=0)
        s = _dot_nt(qst, k_m)
        ok = None
        if gs > 1:
            ok = band
        if not prev_valid and start < WINDOW:
            valid = k_idx + start >= WINDOW
            ok = valid if ok is None else ok & valid
        if ok is not None:
            s = jnp.where(ok, s, NEG_INF)
        sink = jnp.where(first_pair, sinks_ref[kv * GROUP_A + half], sinks_ref[kv * GROUP_A + 2 + half])
        m = jnp.maximum(jnp.max(s, axis=-1, keepdims=True), sink)
        p = jnp.exp2(s - m)
        den = jnp.sum(p, axis=-1, keepdims=True) + jnp.exp2(sink - m)
        st[g, kv, half] = (p.astype(BF16), 1.0 / den)

    def values(g, kv, half):
        base = kv * GROUP_A * HEAD_DIM_A
        start = g * gs * CHUNK
        p, rden = st.pop((g, kv, half))
        v_m = kv_ops[kv][half][1][start:start + n_keys]
        o = jnp.dot(p, v_m, preferred_element_type=F32) * rden
        if half == 0:
            st[g, kv] = o
            return
        acc = st.pop((g, kv)) + o
        for c in range(gs):
            rows = slice((g * gs + c) * CHUNK, (g * gs + c + 1) * CHUNK)
            for pair in range(2):
                sl = slice(base + pair * LANES, base + (pair + 1) * LANES)
                blk = acc[(2 * c + pair) * CHUNK:(2 * c + pair + 1) * CHUNK]
                o_ref[rows, sl] = (blk * gate_ref[rows, sl].astype(F32)).astype(o_ref.dtype)

    for thunk in _lagged([functools.partial(scores, *u) for u in units],
                         [functools.partial(values, *u) for u in units], 2):
        thunk()


def _attn_kernel(*refs, n_ch, has_hist):
    if has_hist:
        _attn_body(*refs, n_ch=n_ch, prev_valid=True)
        return
    t = pl.program_id(1)

    @pl.when(t == 0)
    def _():
        _attn_body(*refs, n_ch=n_ch, prev_valid=False)

    @pl.when(t > 0)
    def _():
        _attn_body(*refs, n_ch=n_ch, prev_valid=True)


def _attention(sinks2, q, k, v, gate, k_hist, v_hist, bsz, n_c, n_ch):
    n = q.shape[0]
    n_t = n_c // n_ch
    tok = n_ch * CHUNK
    cur = lambda w: pl.BlockSpec((tok, w), lambda b, t: (b * n_t + t, 0))
    if k_hist is None:
        per_seq = n_c * CHUNK // WINDOW
        prev = pl.BlockSpec((WINDOW, KV_WIDTH_A),
                            lambda b, t: (b * per_seq + jnp.maximum(t * (tok // WINDOW) - 1, 0), 0))
        kp, vp = k, v
    else:
        assert n_t == 1
        prev = pl.BlockSpec((WINDOW, KV_WIDTH_A), lambda b, t: (b, 0))
        kp, vp = k_hist, v_hist
    return pl.pallas_call(
        functools.partial(_attn_kernel, n_ch=n_ch, has_hist=k_hist is not None),
        grid=(bsz, n_t),
        in_specs=[pl.BlockSpec(memory_space=pltpu.SMEM),
                  cur(WIDTH_A), prev, cur(KV_WIDTH_A), prev, cur(KV_WIDTH_A), cur(WIDTH_A)],
        out_specs=cur(WIDTH_A),
        out_shape=jax.ShapeDtypeStruct((n, WIDTH_A), BF16),
        compiler_params=pltpu.CompilerParams(dimension_semantics=("parallel", "parallel"),
                                             vmem_limit_bytes=VMEM_LIMIT),
        name="band_attention",
    )(sinks2, q, kp, k, vp, v, gate)


def _lane_col(x, lane_idx, j):
    return jnp.sum(jnp.where(lane_idx == j, x, 0.0), axis=-1, keepdims=True)


def _delta_kernel(*refs, has_state, n_ch):
    if has_state:
        cs_ref, ab_ref, gate_ref, alog_ref, dtb_ref, nw_ref, s0_ref, o_ref, s_ref = refs
    else:
        cs_ref, ab_ref, gate_ref, alog_ref, dtb_ref, nw_ref, o_ref, s_ref = refs

    @pl.when(pl.program_id(1) == 0)
    def _():
        if has_state:
            s_ref[...] = s0_ref[...]
        else:
            s_ref[...] = jnp.zeros_like(s_ref)

    ri = lax.broadcasted_iota(jnp.int32, (CHUNK, CHUNK), 0)
    ci = lax.broadcasted_iota(jnp.int32, (CHUNK, CHUNK), 1)
    causal = ri >= ci
    strict = ri > ci
    eye = jnp.where(ri == ci, 1.0, 0.0).astype(F32)
    triu = jnp.where(ri <= ci, 1.0, 0.0).astype(F32)
    lane_idx = lax.broadcasted_iota(jnp.int32, (CHUNK, LANES), 1)
    sub_idx = lax.broadcasted_iota(jnp.int32, (N_HEADS_B, CHUNK), 1)
    heads = range(N_HEADS_B)
    ops = {(c, h): {} for c in range(n_ch) for h in heads}
    gates = [None] * n_ch

    def gate_stage(c):
        ab_t = ab_ref[c * CHUNK:(c + 1) * CHUNK, :].T
        g_t = -jnp.exp(alog_ref[...]) * _softplus(ab_t[0:N_HEADS_B] + dtb_ref[...])
        gc_t = jnp.dot(g_t, triu, preferred_element_type=F32, precision=lax.Precision.HIGHEST)
        gl = jnp.sum(jnp.where(sub_idx == CHUNK - 1, gc_t, 0.0), axis=-1, keepdims=True)
        pad = jnp.zeros((LANES - 5 * N_HEADS_B, CHUNK), F32)
        packed = jnp.concatenate([gc_t, jnp.exp(gc_t), jnp.exp(gl - gc_t), -_sigmoid(ab_t[N_HEADS_B:2 * N_HEADS_B]),
                                  jnp.broadcast_to(jnp.exp(gl), (N_HEADS_B, CHUNK)), pad], axis=0)
        gates[c] = dict(gc_t=gc_t, cols=packed.T)

    def prep(c, h):
        gt, op = gates[c], ops[c, h]
        col = lambda i: _lane_col(gt["cols"], lane_idx, N_HEADS_B * i + h)
        rows = slice(c * CHUNK, (c + 1) * CHUNK)
        slab = lambda j: cs_ref[rows, j * LANES:(j + 1) * LANES].astype(F32)
        q, k, v = slab(h), slab(N_HEADS_B + h), slab(2 * N_HEADS_B + h)
        q = q * (lax.rsqrt(jnp.sum(q * q, axis=-1, keepdims=True) + L2_EPS) * (HEAD_DK_B ** -0.5))
        k = k * lax.rsqrt(jnp.sum(k * k, axis=-1, keepdims=True) + L2_EPS)
        gc_col, e_gc, e_rev, nb, e_gl = col(0), col(1), col(2), col(3), col(4)
        gc_row = gt["gc_t"][h:h + 1, :]
        kbn = k * nb
        op.update(
            lhs_a=jnp.concatenate([kbn.astype(BF16), q.astype(BF16)], axis=0),
            k=k.astype(BF16),
            rhs_uw=jnp.concatenate([(v * -nb).astype(BF16), (kbn * e_gc).astype(BF16)], axis=1),
            q_dec=(q * e_gc).astype(BF16),
            k_dec_t=(k * e_rev).T.astype(BF16),
            decay=jnp.where(causal, jnp.exp(jnp.where(causal, gc_col - gc_row, 0.0)), 0.0),
            e_gl=jnp.concatenate([e_gl, e_gl], axis=0))

    def scores(c, h):
        op = ops[c, h]
        aq = _dot_nt(op["lhs_a"], op["k"])
        op["p"] = jnp.where(strict, aq[:CHUNK] * op["decay"], 0.0)
        op["qk"] = (aq[CHUNK:] * op["decay"]).astype(BF16)

    def inv_first(c, h):
        op = ops[c, h]
        op["t"] = eye + op["p"]
        pb = op["p"].astype(BF16)
        op["p"] = jnp.dot(pb, pb, preferred_element_type=F32)

    def inv_mid(c, h):
        op = ops[c, h]
        pb = op["p"].astype(BF16)
        r = jnp.dot(jnp.concatenate([pb, op["t"].astype(BF16)], axis=0), pb, preferred_element_type=F32)
        op["p"] = r[:CHUNK]
        op["t"] = op["t"] + r[CHUNK:]

    def inv_last(c, h):
        op = ops[c, h]
        op["t"] = op["t"] + _dot(op["t"], op["p"])

    def apply_inv(c, h):
        op = ops[c, h]
        uw = _dot(op["t"], op["rhs_uw"])
        op["u"] = uw[:, :HEAD_DV_B]
        op["wq"] = jnp.concatenate([uw[:, HEAD_DV_B:].astype(BF16), op["q_dec"]], axis=0)

    state = [None] * N_HEADS_B

    def load_state(h):
        state[h] = s_ref[0, h]

    def rec_first(c, h):
        op = ops[c, h]
        ws = _dot(op["wq"], state[h])
        op["v_new"] = (op["u"] + ws[:CHUNK]).astype(BF16)
        op["qs"] = ws[CHUNK:]

    def rec_second(c, h):
        op = ops[c, h]
        rows = slice(c * CHUNK, (c + 1) * CHUNK)
        hs = slice(h * HEAD_DV_B, (h + 1) * HEAD_DV_B)
        r = jnp.dot(jnp.concatenate([op["qk"], op["k_dec_t"]], axis=0), op["v_new"],
                    preferred_element_type=F32)
        o = op["qs"] + r[:CHUNK]
        state[h] = state[h] * op["e_gl"] + r[CHUNK:]
        o = o * lax.rsqrt(jnp.mean(o * o, axis=-1, keepdims=True) + RMS_EPS) * nw_ref[...]
        o_ref[rows, hs] = (o * gate_ref[rows, hs].astype(F32)).astype(o_ref.dtype)

    def per_head(fn, c):
        return [functools.partial(fn, c, h) for h in heads]

    def prep_stream(c):
        return [functools.partial(gate_stage, c)] + per_head(prep, c)

    def inverse_stream(c):
        levels = [scores, inv_first] + [inv_mid] * 4 + [inv_last, apply_inv]
        return [t for fn in levels for t in per_head(fn, c)]

    def recurrence_stream(c):
        return per_head(rec_first, c) + per_head(rec_second, c)

    for h in heads:
        load_state(h)
    for step in range(n_ch + 2):
        streams = []
        if step - 2 >= 0:
            streams.append(recurrence_stream(step - 2))
        if 0 <= step - 1 < n_ch:
            streams.append(inverse_stream(step - 1))
        if step < n_ch:
            streams.append(prep_stream(step))
        for thunk in _interleave(streams):
            thunk()
    for h in heads:
        s_ref[0, h] = state[h]


def _delta(cs, ab, gate, alog_col, dtb_col, nw_row, s0, bsz, n_c, n_ch):
    n = cs.shape[0]
    n_t = n_c // n_ch
    cur = lambda w: pl.BlockSpec((n_ch * CHUNK, w), lambda b, t: (b * n_t + t, 0))
    const = lambda shape: pl.BlockSpec(shape, lambda b, t: (0,) * len(shape))
    state = pl.BlockSpec((1, N_HEADS_B, HEAD_DK_B, HEAD_DV_B), lambda b, t: (b, 0, 0, 0))
    in_specs = [cur(CONV_CH), cur(AB_PAD), cur(WIDTH_B),
                const((N_HEADS_B, 1)), const((N_HEADS_B, 1)), const((1, HEAD_DV_B))]
    args = [cs, ab, gate, alog_col, dtb_col, nw_row]
    has_state = s0 is not None
    if has_state:
        in_specs.append(state)
        args.append(s0)
    return pl.pallas_call(
        functools.partial(_delta_kernel, has_state=has_state, n_ch=n_ch),
        grid=(bsz, n_t),
        in_specs=in_specs,
        out_specs=[cur(WIDTH_B), state],
        out_shape=[jax.ShapeDtypeStruct((n, WIDTH_B), BF16),
                   jax.ShapeDtypeStruct((bsz, N_HEADS_B, HEAD_DK_B, HEAD_DV_B), F32)],
        compiler_params=pltpu.CompilerParams(dimension_semantics=("parallel", "arbitrary"),
                                             vmem_limit_bytes=VMEM_LIMIT),
        name="gated_delta",
    )(*args)


def _out_kernel(oa_ref, ob_ref, sga_ref, sgb_ref, x_ref, woa_ref, wob_ref, wout_ref, lng_ref, lnb_ref, y_ref):
    y_a = jnp.dot(oa_ref[...], woa_ref[...], preferred_element_type=F32)
    y_b = jnp.dot(ob_ref[...], wob_ref[...], preferred_element_type=F32)
    hmix = sga_ref[...].astype(F32) * y_a + sgb_ref[...].astype(F32) * y_b
    sub = jnp.dot(hmix.astype(BF16), wout_ref[...], preferred_element_type=F32)
    r = ALPHA * x_ref[...] + sub
    mu = jnp.mean(r, axis=-1, keepdims=True)
    d = r - mu
    var = jnp.mean(d * d, axis=-1, keepdims=True)
    y_ref[...] = d * lax.rsqrt(var + LN_EPS) * lng_ref[...] + lnb_ref[...]


def _output(oa, ob, sga, sgb, x2d, woa, wob, wout, lng, lnb, tm):
    n = x2d.shape[0]
    row = pl.BlockSpec((tm, D_MODEL), lambda i: (i, 0))
    wspec = pl.BlockSpec((D_MODEL, D_MODEL), lambda i: (0, 0))
    vec = pl.BlockSpec((1, D_MODEL), lambda i: (0, 0))
    return pl.pallas_call(
        _out_kernel,
        grid=(n // tm,),
        in_specs=[row, row, row, row, row, wspec, wspec, wspec, vec, vec],
        out_specs=row,
        out_shape=jax.ShapeDtypeStruct((n, D_MODEL), F32),
        compiler_params=pltpu.CompilerParams(dimension_semantics=("parallel",),
                                             vmem_limit_bytes=VMEM_LIMIT),
        name="out_projection",
    )(oa, ob, sga, sgb, x2d, woa, wob, wout, lng, lnb)


def _rope_tables(pos):
    half = ROT_DIM // 2
    inv_freq = jnp.power(ROPE_THETA, -jnp.arange(half, dtype=F32) / half)
    ang = pos[:, None] * inv_freq[None, :]
    cos, sin = jnp.cos(ang), jnp.sin(ang)
    n = pos.shape[0]
    ones = jnp.ones((n, HEAD_DIM_A - ROT_DIM), F32)
    zeros = jnp.zeros((n, HEAD_DIM_A - ROT_DIM), F32)
    zh = jnp.zeros((n, half), F32)
    ct = jnp.concatenate([cos, cos, ones], axis=1)
    sa = jnp.concatenate([zh, sin, zeros], axis=1)
    sb = jnp.concatenate([-sin, zh, zeros], axis=1)
    tile2 = lambda t: jnp.concatenate([t, t], axis=1)
    return tile2(ct), tile2(sa), tile2(sb)


def _layer(x, pos_offset, k_hist, v_hist, conv_hist, s0, weights):
    (w_r, sinks2, conv_w, alog_col, dtb_col, nw_row, woa, wob, wout, lng, lnb) = weights
    bsz, l_len, _ = x.shape
    n = bsz * l_len
    n_c = l_len // CHUNK
    tm = min(PROMPT_TILE, l_len)
    n_ch = tm // CHUNK
    x2d = x.reshape(n, D_MODEL)
    pos = jnp.arange(l_len, dtype=F32) + pos_offset
    ct, sa, sb = _rope_tables(pos)
    if k_hist is None:
        kh = vh = hist8 = None
    else:
        assert l_len == tm
        kh = k_hist.reshape(bsz * WINDOW, KV_WIDTH_A)
        vh = v_hist.reshape(bsz * WINDOW, KV_WIDTH_A)
        hist8 = jnp.pad(conv_hist, ((0, 0), (SUBLANES - (CONV_W - 1), 0), (0, 0)))
    q, k, v, gza, cs, gzb, sga, sgb, ab, tail = _projection(x2d, hist8, w_r, ct, sa, sb, conv_w, tm, l_len // tm)
    oa = _attention(sinks2, q, k, v, gza, kh, vh, bsz, n_c, n_ch)
    ob, s_new = _delta(cs, ab, gzb, alog_col, dtb_col, nw_row, s0, bsz, n_c, n_ch)
    y = _output(oa, ob, sga, sgb, x2d, woa, wob, wout, lng, lnb, min(OUT_TILE, n))

    k3 = k.reshape(bsz, l_len, N_KV_A, HEAD_DIM_A)
    v3 = v.reshape(bsz, l_len, N_KV_A, HEAD_DIM_A)
    if k_hist is None:
        new_k, new_v = k3[:, l_len - WINDOW:], v3[:, l_len - WINDOW:]
    else:
        keep = k_hist.shape[1]
        new_k = jnp.concatenate([k_hist, k3], axis=1)[:, l_len:]
        new_v = jnp.concatenate([v_hist, v3], axis=1)[:, l_len:]
        assert new_k.shape[1] == keep == WINDOW
    new_conv = tail.reshape(bsz, l_len // tm, SUBLANES, CONV_CH)[:, -1, SUBLANES - (CONV_W - 1):]
    return y.reshape(bsz, l_len, D_MODEL), new_k, new_v, new_conv, s_new


def kernel(x_prompt, x_sample, cache_attn_k, cache_attn_v, state_conv, state_delta, w_in, attn_sinks, conv_w,
           a_log, dt_bias, delta_norm_w, w_o_attn, w_o_delta, w_out, ln_g, ln_b):
    splits = np.cumsum((WIDTH_A, KV_WIDTH_A, KV_WIDTH_A, WIDTH_A, CONV_CH, N_HEADS_B, N_HEADS_B, WIDTH_B,
                        D_MODEL, D_MODEL))
    ab_lo, ab_hi = int(splits[4]), int(splits[6])
    w_r = jnp.concatenate([w_in[:, :ab_lo], w_in[:, ab_hi:], w_in[:, ab_lo:ab_hi],
                           jnp.zeros((D_MODEL, AB_PAD - (ab_hi - ab_lo)), w_in.dtype)], axis=1).astype(BF16)
    head_col = lambda t: t.astype(F32).reshape(N_HEADS_B, 1)
    weights = (w_r, attn_sinks.astype(F32) * LOG2E, conv_w.astype(F32), head_col(a_log), head_col(dt_bias),
               delta_norm_w.astype(F32).reshape(1, HEAD_DV_B), w_o_attn.astype(BF16), w_o_delta.astype(BF16),
               w_out.astype(BF16), ln_g.astype(F32).reshape(1, D_MODEL), ln_b.astype(F32).reshape(1, D_MODEL))
    yp, kp, vp, cp, sp = _layer(x_prompt, 0.0, None, None, None, None, weights)
    ys, ks, vs, cs, ss = _layer(x_sample, float(PAST_LEN), cache_attn_k, cache_attn_v, state_conv, state_delta,
                                weights)
    return (yp, ys, kp, vp, cp, sp, ks, vs, cs, ss)
```

```python
import functools
import math

import numpy as np
import jax
import jax.numpy as jnp
from jax import lax
from jax.experimental import pallas as pl
from jax.experimental.pallas import tpu as pltpu

D_MODEL = 1024
CHUNK = 64
N_HEADS_A = 16
N_KV_A = 4
HEAD_DIM_A = 64
GROUP_A = N_HEADS_A // N_KV_A
WINDOW = 128
ROT_DIM = HEAD_DIM_A // 4
ROPE_THETA = 500000.0
WIDTH_A = N_HEADS_A * HEAD_DIM_A
KV_WIDTH_A = N_KV_A * HEAD_DIM_A
N_HEADS_B = 8
HEAD_DK_B = 128
HEAD_DV_B = 128
CONV_W = 4
QK_WIDTH_B = N_HEADS_B * HEAD_DK_B
WIDTH_B = N_HEADS_B * HEAD_DV_B
CONV_CH = 2 * QK_WIDTH_B + WIDTH_B
DEPTH = 1
PAST_LEN = 1024
ALPHA = (2.0 * DEPTH) ** 0.25
LN_EPS = 1e-5
RMS_EPS = 1e-6
L2_EPS = 1e-6
NEG_INF = -1e30
LOG2E = math.log2(math.e)

LANES = 128
SUBLANES = 8
AB_PAD = LANES
OFF_Q = 0
OFF_K = OFF_Q + WIDTH_A
OFF_V = OFF_K + KV_WIDTH_A
OFF_ZA = OFF_V + KV_WIDTH_A
OFF_QKVB = OFF_ZA + WIDTH_A
OFF_ZB = OFF_QKVB + CONV_CH
OFF_GA = OFF_ZB + WIDTH_B
OFF_GB = OFF_GA + D_MODEL
OFF_AB = OFF_GB + D_MODEL
D_IN_PAD = OFF_AB + AB_PAD

VMEM_LIMIT = 56 * 1024 * 1024
PROJ_SEG = 512
EPI_WEIGHT_CONV, EPI_WEIGHT_ROPE, EPI_WEIGHT_ACT = 11, 16, 9
PROJ_TILE = 512
ATTN_TILE = 256
DELTA_TILE = 512
DELTA_GROUP = 2
OUT_COLS = 256

F32 = jnp.float32
BF16 = jnp.bfloat16


def _sigmoid(x):
    return 0.5 + 0.5 * jnp.tanh(0.5 * x)


def _silu(x):
    h = 0.5 * x
    return h + h * jnp.tanh(h)


def _softplus(x):
    return jnp.maximum(x, 0.0) + jnp.log1p(jnp.exp(-jnp.abs(x)))


def _dot(a, b):
    return jnp.dot(a.astype(BF16), b.astype(BF16), preferred_element_type=F32)


def _dot_nt(a, b):
    return lax.dot_general(a.astype(BF16), b.astype(BF16), (((1,), (1,)), ((), ())),
                           preferred_element_type=F32)


def _interleave(streams):
    tagged = [((i + 0.5) / len(s), si, i, t) for si, s in enumerate(streams) for i, t in enumerate(s)]
    tagged.sort(key=lambda x: x[:3])
    return [t for *_, t in tagged]


def _spread(mm, epi):
    total = sum(w for e in epi for w, _ in e)
    order, queue, emitted = [], [], 0
    for i, m in enumerate(mm):
        order.append(m)
        if i > 0:
            queue.extend(epi[i - 1])
        while queue and emitted < total * i / len(mm):
            w, thunk = queue.pop(0)
            order.append(thunk)
            emitted += w
    queue.extend(epi[-1])
    order.extend(thunk for _, thunk in queue)
    return order


def _lagged(first, second, lag):
    order = []
    for i in range(len(first) + lag):
        if i < len(first):
            order.append(first[i])
        if 0 <= i - lag < len(second):
            order.append(second[i - lag])
    return order


def _rope_cols(t, ct, sa, sb):
    return t * ct + pltpu.roll(t, 8, axis=1) * sa + pltpu.roll(t, LANES - 8, axis=1) * sb


def _proj_kernel(*refs, tm, tiles_per_seq, has_hist):
    if has_hist:
        (x_ref, hist_ref, w_ref, ct_ref, sa_ref, sb_ref, cw_ref,
         q_ref, k_ref, v_ref, gza_ref, cs_ref, gzb_ref, sga_ref, sgb_ref, ab_ref, tail_ref) = refs
        xb = x_ref[...].astype(BF16)
        xb_look = xb
    else:
        (x_ref, xprev_ref, w_ref, ct_ref, sa_ref, sb_ref, cw_ref,
         q_ref, k_ref, v_ref, gza_ref, cs_ref, gzb_ref, sga_ref, sgb_ref, ab_ref, tail_ref) = refs
        xb = x_ref[...].astype(BF16)
        xb_look = jnp.concatenate([xb, xprev_ref[...].astype(BF16)], axis=0)
        first_tile = pl.program_id(0) % tiles_per_seq == 0
    ct, sa, sb = ct_ref[...], sa_ref[...], sb_ref[...]
    res = {}

    def matmul(key, off, width, lhs):
        res[key] = jnp.dot(lhs, w_ref[:, off:off + width], preferred_element_type=F32)

    def slabs(width):
        return [slice(j * LANES, (j + 1) * LANES) for j in range(width // LANES)]

    def epi_q(key, off):
        r = res.pop(key)
        for sl in slabs(r.shape[1]):
            qj = _rope_cols(r[:, sl], ct, sa, sb) * (HEAD_DIM_A ** -0.5 * LOG2E)
            q_ref[:, off + sl.start:off + sl.stop] = qj.astype(q_ref.dtype)

    def epi_kv(key):
        r = res.pop(key)
        for sl in slabs(KV_WIDTH_A):
            k_ref[:, sl] = _rope_cols(r[:, sl], ct, sa, sb)
        v_ref[...] = r[:, KV_WIDTH_A:]

    def epi_act(key, fn, out_ref, off):
        r = res.pop(key)
        out_ref[:, off:off + r.shape[1]] = fn(r).astype(out_ref.dtype)

    def epi_conv(key, off, sl):
        r = res[key]
        cols = slice(off + sl.start, off + sl.stop)
        tail_ref[:, cols] = r[tm - SUBLANES:tm, sl]
        if has_hist:
            look = hist_ref[0, :, cols]
        else:
            look = jnp.where(first_tile, 0.0, r[tm:tm + SUBLANES, sl])
        cur = r[0:tm, sl]
        xp = jnp.concatenate([look, cur], axis=0)
        acc = cur * cw_ref[CONV_W - 1:CONV_W, cols]
        for j in range(1, CONV_W):
            acc = acc + pltpu.roll(xp, j, axis=0)[SUBLANES:] * cw_ref[CONV_W - 1 - j:CONV_W - j, cols]
        cs_ref[:, cols] = _silu(acc).astype(cs_ref.dtype)

    def epi_ab(key):
        ab_ref[...] = res.pop(key)

    mm, epi = [], []

    def add(key, off, width, epilogues, lhs=None):
        mm.append(functools.partial(matmul, key, off, width, xb if lhs is None else lhs))
        epi.append(epilogues)

    for j in range(CONV_CH // PROJ_SEG):
        add(("qkvb", j), OFF_QKVB + j * PROJ_SEG, PROJ_SEG,
            [(EPI_WEIGHT_CONV, functools.partial(epi_conv, ("qkvb", j), j * PROJ_SEG, sl))
             for sl in slabs(PROJ_SEG)], lhs=xb_look)
    for j in range(WIDTH_A // PROJ_SEG):
        add(("q", j), OFF_Q + j * PROJ_SEG, PROJ_SEG,
            [(EPI_WEIGHT_ROPE, functools.partial(epi_q, ("q", j), j * PROJ_SEG))])
    add("kv", OFF_K, 2 * KV_WIDTH_A, [(EPI_WEIGHT_ROPE // 2, functools.partial(epi_kv, "kv"))])
    for name, base, fn, out_ref in (("za", OFF_ZA, _silu, gza_ref), ("zb", OFF_ZB, _silu, gzb_ref),
                                    ("ga", OFF_GA, _sigmoid, sga_ref), ("gb", OFF_GB, _sigmoid, sgb_ref)):
        for j in range(D_MODEL // PROJ_SEG):
            add((name, j), base + j * PROJ_SEG, PROJ_SEG,
                [(EPI_WEIGHT_ACT, functools.partial(epi_act, (name, j), fn, out_ref, j * PROJ_SEG))])
    add("ab", OFF_AB, AB_PAD, [(1, functools.partial(epi_ab, "ab"))])
    for thunk in _spread(mm, epi):
        thunk()


def _projection(x2d, hist8, w_r, ct, sa, sb, conv_w, tm, tiles_per_seq):
    n = x2d.shape[0]
    n_tiles = n // tm
    t_rows = ct.shape[0]
    n_t = t_rows // tm
    has_hist = hist8 is not None
    row = lambda w: pl.BlockSpec((tm, w), lambda i: (i, 0))
    tab = pl.BlockSpec((tm, LANES), lambda i: (i % n_t, 0))
    if has_hist:
        assert tiles_per_seq == 1
        look_spec = pl.BlockSpec((1, SUBLANES, CONV_CH), lambda i: (i, 0, 0))
        look = hist8
    else:
        blocks = tm // SUBLANES
        look_spec = pl.BlockSpec((SUBLANES, D_MODEL), lambda i: (jnp.maximum(i * blocks - 1, 0), 0))
        look = x2d
    widths = (WIDTH_A, KV_WIDTH_A, KV_WIDTH_A, WIDTH_A, CONV_CH, WIDTH_B, D_MODEL, D_MODEL, AB_PAD)
    dtypes = (BF16, F32, F32, BF16, BF16, BF16, BF16, BF16, F32)
    return pl.pallas_call(
        functools.partial(_proj_kernel, tm=tm, tiles_per_seq=tiles_per_seq, has_hist=has_hist),
        grid=(n_tiles,),
        in_specs=[row(D_MODEL), look_spec,
                  pl.BlockSpec((D_MODEL, D_IN_PAD), lambda i: (0, 0), pipeline_mode=pl.Buffered(1)),
                  tab, tab, tab, pl.BlockSpec((CONV_W, CONV_CH), lambda i: (0, 0))],
        out_specs=[row(w) for w in widths] + [pl.BlockSpec((SUBLANES, CONV_CH), lambda i: (i, 0))],
        out_shape=[jax.ShapeDtypeStruct((n, w), d) for w, d in zip(widths, dtypes)]
        + [jax.ShapeDtypeStruct((n_tiles * SUBLANES, CONV_CH), F32)],
        compiler_params=pltpu.CompilerParams(dimension_semantics=("parallel",),
                                             vmem_limit_bytes=VMEM_LIMIT),
        name="in_projection",
    )(x2d, look, w_r, ct, sa, sb, conv_w)


def _attn_body(sinks_ref, q_ref, kprev_ref, kcur_ref, vprev_ref, vcur_ref, gate_ref, o_ref,
               *, n_ch, prev_valid):
    gs = 2 if n_ch % 2 == 0 else 1
    n_keys = (gs + 2) * CHUNK
    rows_u = gs * 2 * CHUNK
    kext = jnp.concatenate([kprev_ref[...], kcur_ref[...]], axis=0)
    vext = jnp.concatenate([vprev_ref[...], vcur_ref[...]], axis=0)
    lo = lax.broadcasted_iota(jnp.int32, (kext.shape[0], LANES), 1) < HEAD_DIM_A
    r_idx = lax.broadcasted_iota(jnp.int32, (rows_u, n_keys), 0)
    k_idx = lax.broadcasted_iota(jnp.int32, (rows_u, n_keys), 1)
    first_pair = (lax.broadcasted_iota(jnp.int32, (rows_u, 1), 0) // CHUNK) % 2 == 0
    ci = r_idx // (2 * CHUNK)
    band = (k_idx >= ci * CHUNK) & (k_idx < (ci + 3) * CHUNK)

    kv_ops = []
    for kv in range(N_KV_A):
        col, odd = kv // 2, kv % 2
        kc = kext[:, col * LANES:(col + 1) * LANES]
        vc = vext[:, col * LANES:(col + 1) * LANES]
        ks = pltpu.roll(kc, HEAD_DIM_A, axis=1)
        vs = pltpu.roll(vc, HEAD_DIM_A, axis=1)
        if odd:
            kc, ks, vc, vs = ks, kc, vs, vc
        kv_ops.append(((jnp.where(lo, kc, 0.0).astype(BF16), jnp.where(lo, vc, 0.0).astype(BF16)),
                       (jnp.where(lo, 0.0, ks).astype(BF16), jnp.where(lo, 0.0, vs).astype(BF16))))

    units = [(g, kv, half) for g in range(n_ch // gs) for kv in range(N_KV_A) for half in range(2)]
    st = {}

    def scores(g, kv, half):
        base = kv * GROUP_A * HEAD_DIM_A
        start = g * gs * CHUNK
        k_m = kv_ops[kv][half][0][start:start + n_keys]
        qst = jnp.concatenate(
            [q_ref[(g * gs + c) * CHUNK:(g * gs + c + 1) * CHUNK, base + pair * LANES:base + (pair + 1) * LANES]
             for c in range(gs) for pair in range(2)], axis=0)
        s = _dot_nt(qst, k_m)
        ok = None
        if gs > 1:
            ok = band
        if not prev_valid and start < WINDOW:
            valid = k_idx + start >= WINDOW
            ok = valid if ok is None else ok & valid
        if ok is not None:
            s = jnp.where(ok, s, NEG_INF)
        sink = jnp.where(first_pair, sinks_ref[kv * GROUP_A + half], sinks_ref[kv * GROUP_A + 2 + half])
        m = jnp.maximum(jnp.max(s, axis=-1, keepdims=True), sink)
        p = jnp.exp2(s - m)
        den = jnp.sum(p, axis=-1, keepdims=True) + jnp.exp2(sink - m)
        st[g, kv, half] = (p.astype(BF16), 1.0 / den)

    def values(g, kv, half):
        base = kv * GROUP_A * HEAD_DIM_A
        start = g * gs * CHUNK
        p, rden = st.pop((g, kv, half))
        v_m = kv_ops[kv][half][1][start:start + n_keys]
        o = jnp.dot(p, v_m, preferred_element_type=F32) * rden
        if half == 0:
            st[g, kv] = o
            return
        acc = st.pop((g, kv)) + o
        for c in range(gs):
            rows = slice((g * gs + c) * CHUNK, (g * gs + c + 1) * CHUNK)
            for pair in range(2):
                sl = slice(base + pair * LANES, base + (pair + 1) * LANES)
                blk = acc[(2 * c + pair) * CHUNK:(2 * c + pair + 1) * CHUNK]
                o_ref[rows, sl] = (blk * gate_ref[rows, sl].astype(F32)).astype(o_ref.dtype)

    for thunk in _lagged([functools.partial(scores, *u) for u in units],
                         [functools.partial(values, *u) for u in units], 2):
        thunk()


def _attn_kernel(*refs, n_ch, has_hist):
    if has_hist:
        _attn_body(*refs, n_ch=n_ch, prev_valid=True)
        return
    t = pl.program_id(1)

    @pl.when(t == 0)
    def _():
        _attn_body(*refs, n_ch=n_ch, prev_valid=False)

    @pl.when(t > 0)
    def _():
        _attn_body(*refs, n_ch=n_ch, prev_valid=True)


def _attention(sinks2, q, k, v, gate, k_hist, v_hist, bsz, n_c, n_ch):
    n = q.shape[0]
    n_t = n_c // n_ch
    tok = n_ch * CHUNK
    cur = lambda w: pl.BlockSpec((tok, w), lambda b, t: (b * n_t + t, 0))
    if k_hist is None:
        per_seq = n_c * CHUNK // WINDOW
        prev = pl.BlockSpec((WINDOW, KV_WIDTH_A),
                            lambda b, t: (b * per_seq + jnp.maximum(t * (tok // WINDOW) - 1, 0), 0))
        kp, vp = k, v
    else:
        assert n_t == 1
        prev = pl.BlockSpec((WINDOW, KV_WIDTH_A), lambda b, t: (b, 0))
        kp, vp = k_hist, v_hist
    return pl.pallas_call(
        functools.partial(_attn_kernel, n_ch=n_ch, has_hist=k_hist is not None),
        grid=(bsz, n_t),
        in_specs=[pl.BlockSpec(memory_space=pltpu.SMEM),
                  cur(WIDTH_A), prev, cur(KV_WIDTH_A), prev, cur(KV_WIDTH_A), cur(WIDTH_A)],
        out_specs=cur(WIDTH_A),
        out_shape=jax.ShapeDtypeStruct((n, WIDTH_A), BF16),
        compiler_params=pltpu.CompilerParams(dimension_semantics=("parallel", "parallel"),
                                             vmem_limit_bytes=VMEM_LIMIT),
        name="band_attention",
    )(sinks2, q, kp, k, vp, v, gate)


def _lane_col(x, lane_idx, j):
    return jnp.sum(jnp.where(lane_idx == j, x, 0.0), axis=-1, keepdims=True)


def _delta_kernel(*refs, has_state, n_ch):
    (cs_ref, ab_ref, gate_ref, alog_ref, dtb_ref, nw_ref,
     oa_ref, sga_ref, sgb_ref, x_ref, woa_ref, wob_ref, wout_ref, lng_ref, lnb_ref) = refs[:15]
    if has_state:
        s0_ref, y_ref, s_ref, ob_ref = refs[15:]
    else:
        y_ref, s_ref, ob_ref = refs[15:]

    @pl.when(pl.program_id(1) == 0)
    def _():
        if has_state:
            s_ref[...] = s0_ref[...]
        else:
            s_ref[...] = jnp.zeros_like(s_ref)

    ri = lax.broadcasted_iota(jnp.int32, (CHUNK, LANES), 0)
    li = lax.broadcasted_iota(jnp.int32, (CHUNK, LANES), 1)
    ci = li % CHUNK
    lo = li < CHUNK
    lo_row = lax.broadcasted_iota(jnp.int32, (1, LANES), 1) < CHUNK
    causal = ri >= ci
    strict = ri > ci
    eye2 = jnp.where(ri == ci, 1.0, 0.0).astype(F32)
    si = lax.broadcasted_iota(jnp.int32, (LANES, LANES), 0)
    sl = lax.broadcasted_iota(jnp.int32, (LANES, LANES), 1)
    cum2 = jnp.where((si < CHUNK) & (si <= sl % CHUNK), 1.0, 0.0).astype(F32)
    last_lane = lax.broadcasted_iota(jnp.int32, (N_HEADS_B, LANES), 1) == CHUNK - 1
    heads = range(N_HEADS_B)
    pairs = range(N_HEADS_B // 2)
    zero_rhs = jnp.zeros((CHUNK, 2 * HEAD_DV_B), BF16)
    zero_v = jnp.zeros((CHUNK, HEAD_DV_B), BF16)
    zero_k = jnp.zeros((CHUNK, HEAD_DK_B), BF16)
    ops = {(c, p): {} for c in range(n_ch) for p in pairs}
    gates = [None] * n_ch

    def blockdiag(x):
        return jnp.concatenate([jnp.where(lo, x, 0.0), jnp.where(lo, 0.0, x)], axis=0).astype(BF16)

    def gate_stage(c):
        ab = ab_ref[c * CHUNK:(c + 1) * CHUNK, :]
        ab_t = jnp.concatenate([ab, ab], axis=0).T
        g_t = -jnp.exp(alog_ref[...]) * _softplus(ab_t[0:N_HEADS_B] + dtb_ref[...])
        gc_t = jnp.dot(g_t, cum2, preferred_element_type=F32, precision=lax.Precision.HIGHEST)
        gl = jnp.sum(jnp.where(last_lane, gc_t, 0.0), axis=-1, keepdims=True)
        pad = jnp.zeros((LANES - 5 * N_HEADS_B, LANES), F32)
        packed = jnp.concatenate([gc_t, jnp.exp(gc_t), jnp.exp(gl - gc_t), -_sigmoid(ab_t[N_HEADS_B:2 * N_HEADS_B]),
                                  jnp.broadcast_to(jnp.exp(gl), (N_HEADS_B, LANES)), pad], axis=0)
        gates[c] = dict(gc_t=gc_t, cols=packed.T[:CHUNK])

    def prep(c, p):
        gt, op = gates[c], ops[c, p]
        rows = slice(c * CHUNK, (c + 1) * CHUNK)
        slab = lambda j: cs_ref[rows, j * LANES:(j + 1) * LANES].astype(F32)
        per = []
        for h in (2 * p, 2 * p + 1):
            col = lambda i: _lane_col(gt["cols"], li, N_HEADS_B * i + h)
            q, k, v = slab(h), slab(N_HEADS_B + h), slab(2 * N_HEADS_B + h)
            q = q * (lax.rsqrt(jnp.sum(q * q, axis=-1, keepdims=True) + L2_EPS) * (HEAD_DK_B ** -0.5))
            k = k * lax.rsqrt(jnp.sum(k * k, axis=-1, keepdims=True) + L2_EPS)
            gc_col, e_gc, e_rev, nb, e_gl = col(0), col(1), col(2), col(3), col(4)
            kbn = k * nb
            per.append(dict(
                lhs=jnp.concatenate([kbn.astype(BF16), q.astype(BF16)], axis=0),
                k=k.astype(BF16),
                rhs_uw=jnp.concatenate([(v * -nb).astype(BF16), (kbn * e_gc).astype(BF16)], axis=1),
                q_dec=(q * e_gc).astype(BF16), k_dec=k * e_rev, gc_col=gc_col,
                e_gl=jnp.concatenate([e_gl, e_gl], axis=0)))
        a, b = per
        gc_row = jnp.where(lo_row, gt["gc_t"][2 * p:2 * p + 1, :], gt["gc_t"][2 * p + 1:2 * p + 2, :])
        diff = jnp.where(lo, a["gc_col"], b["gc_col"]) - gc_row
        op.update(
            lhs=jnp.concatenate([a["lhs"], b["lhs"]], axis=1),
            k_bd=jnp.concatenate([jnp.concatenate([a["k"], zero_k], axis=1),
                                  jnp.concatenate([zero_k, b["k"]], axis=1)], axis=0),
            rhs_uw=(jnp.concatenate([a["rhs_uw"], zero_rhs], axis=0),
                    jnp.concatenate([zero_rhs, b["rhs_uw"]], axis=0)),
            q_dec=(a["q_dec"], b["q_dec"]),
            k_dec_t=jnp.concatenate([a["k_dec"], b["k_dec"]], axis=0).T.astype(BF16),
            decay=jnp.where(causal, jnp.exp(jnp.where(causal, diff, 0.0)), 0.0),
            e_gl=(a["e_gl"], b["e_gl"]))

    def scores(c, p):
        op = ops[c, p]
        aq = _dot_nt(op["lhs"], op["k_bd"])
        op["p"] = jnp.where(strict, aq[:CHUNK] * op["decay"], 0.0)
        op["qk"] = (aq[CHUNK:] * op["decay"]).astype(BF16)

    def inv_first(c, p):
        op = ops[c, p]
        op["t"] = eye2 + op["p"]
        op["p"] = jnp.dot(op["p"].astype(BF16), blockdiag(op["p"]), preferred_element_type=F32)

    def inv_mid(c, p):
        op = ops[c, p]
        r = jnp.dot(jnp.concatenate([op["p"].astype(BF16), op["t"].astype(BF16)], axis=0), blockdiag(op["p"]),
                    preferred_element_type=F32)
        op["p"] = r[:CHUNK]
        op["t"] = op["t"] + r[CHUNK:]

    def inv_last(c, p):
        op = ops[c, p]
        op["t"] = op["t"] + jnp.dot(op["t"].astype(BF16), blockdiag(op["p"]), preferred_element_type=F32)

    def apply_inv(c, p):
        op = ops[c, p]
        tb = op["t"].astype(BF16)
        op["u"], op["wq"] = [], []
        for j in range(2):
            uw = jnp.dot(tb, op["rhs_uw"][j], preferred_element_type=F32)
            op["u"].append(uw[:, :HEAD_DV_B])
            op["wq"].append(jnp.concatenate([uw[:, HEAD_DV_B:].astype(BF16), op["q_dec"][j]], axis=0))

    state = [None] * N_HEADS_B

    def load_state(h):
        state[h] = s_ref[0, h]

    def rec_first(c, h):
        op, j = ops[c, h // 2], h % 2
        ws = _dot(op["wq"][j], state[h])
        op["v_new", j] = (op["u"][j] + ws[:CHUNK]).astype(BF16)
        op["qs", j] = ws[CHUNK:]

    def rec_second(c, p):
        op = ops[c, p]
        rows = slice(c * CHUNK, (c + 1) * CHUNK)
        v_bd = jnp.concatenate([jnp.concatenate([op["v_new", 0], zero_v], axis=1),
                                jnp.concatenate([zero_v, op["v_new", 1]], axis=1)], axis=0)
        r = jnp.dot(jnp.concatenate([op["qk"], op["k_dec_t"]], axis=0), v_bd,
                    preferred_element_type=F32)
        for j in range(2):
            h = 2 * p + j
            hs = slice(h * HEAD_DV_B, (h + 1) * HEAD_DV_B)
            rj = r[:, j * HEAD_DV_B:(j + 1) * HEAD_DV_B]
            o = op["qs", j] + rj[:CHUNK]
            state[h] = state[h] * op["e_gl"][j] + rj[CHUNK:]
            o = o * lax.rsqrt(jnp.mean(o * o, axis=-1, keepdims=True) + RMS_EPS) * nw_ref[...]
            ob_ref[rows, hs] = (o * gate_ref[rows, hs].astype(F32)).astype(ob_ref.dtype)

    col_blocks = [slice(j * OUT_COLS, (j + 1) * OUT_COLS) for j in range(D_MODEL // OUT_COLS)]
    outs = {}

    def mix(chunks, cb):
        rows = slice(chunks[0] * CHUNK, (chunks[-1] + 1) * CHUNK)
        y_a = jnp.dot(oa_ref[rows, :], woa_ref[:, cb], preferred_element_type=F32)
        y_b = jnp.dot(ob_ref[rows, :], wob_ref[:, cb], preferred_element_type=F32)
        hmix = sga_ref[rows, cb].astype(F32) * y_a + sgb_ref[rows, cb].astype(F32) * y_b
        outs.setdefault(chunks[0], []).append(hmix.astype(BF16))

    def join(chunks):
        outs[chunks[0]] = jnp.concatenate(outs[chunks[0]], axis=1)

    def project(chunks, cb):
        outs.setdefault(("sub", chunks[0]), []).append(
            jnp.dot(outs[chunks[0]], wout_ref[:, cb], preferred_element_type=F32))

    def norm(chunks):
        rows = slice(chunks[0] * CHUNK, (chunks[-1] + 1) * CHUNK)
        del outs[chunks[0]]
        r = ALPHA * x_ref[rows, :] + jnp.concatenate(outs.pop(("sub", chunks[0])), axis=1)
        mu = jnp.mean(r, axis=-1, keepdims=True)
        d = r - mu
        var = jnp.mean(d * d, axis=-1, keepdims=True)
        y_ref[rows, :] = d * lax.rsqrt(var + LN_EPS) * lng_ref[...] + lnb_ref[...]

    def output_stream(chunks):
        return ([functools.partial(mix, chunks, cb) for cb in col_blocks] + [functools.partial(join, chunks)]
                + [functools.partial(project, chunks, cb) for cb in col_blocks] + [functools.partial(norm, chunks)])

    def over(fn, chunks, idx):
        return [functools.partial(fn, c, i) for c in chunks for i in idx]

    def prep_stream(chunks):
        return [functools.partial(gate_stage, c) for c in chunks] + over(prep, chunks, pairs)

    def level_stream(levels):
        return lambda chunks: [t for fn in levels for t in over(fn, chunks, pairs)]

    def recurrence_stream(chunks):
        return [t for c in chunks for t in over(rec_first, [c], heads) + over(rec_second, [c], pairs)]

    stages = [prep_stream,
              level_stream([scores, inv_first, inv_mid]),
              level_stream([inv_mid, inv_mid, inv_mid]),
              level_stream([inv_last, apply_inv]),
              recurrence_stream,
              output_stream]
    group = DELTA_GROUP if n_ch % DELTA_GROUP == 0 else 1
    groups = [list(range(g * group, (g + 1) * group)) for g in range(n_ch // group)]
    for h in heads:
        load_state(h)
    for step in range(len(groups) + len(stages) - 1):
        streams = [stage(groups[step - k]) for k, stage in reversed(list(enumerate(stages)))
                   if 0 <= step - k < len(groups)]
        for thunk in _interleave(streams):
            thunk()
    for h in heads:
        s_ref[0, h] = state[h]


def _delta_out(cs, ab, gate, alog_col, dtb_col, nw_row, oa, sga, sgb, x2d, woa, wob, wout, lng, lnb, s0,
               bsz, n_c, n_ch):
    n = cs.shape[0]
    n_t = n_c // n_ch
    cur = lambda w: pl.BlockSpec((n_ch * CHUNK, w), lambda b, t: (b * n_t + t, 0))
    const = lambda shape: pl.BlockSpec(shape, lambda b, t: (0,) * len(shape))
    weight = pl.BlockSpec((D_MODEL, D_MODEL), lambda b, t: (0, 0), pipeline_mode=pl.Buffered(1))
    state = pl.BlockSpec((1, N_HEADS_B, HEAD_DK_B, HEAD_DV_B), lambda b, t: (b, 0, 0, 0))
    in_specs = [cur(CONV_CH), cur(AB_PAD), cur(WIDTH_B),
                const((N_HEADS_B, 1)), const((N_HEADS_B, 1)), const((1, HEAD_DV_B)),
                cur(WIDTH_A), cur(D_MODEL), cur(D_MODEL), cur(D_MODEL), weight, weight, weight,
                const((1, D_MODEL)), const((1, D_MODEL))]
    args = [cs, ab, gate, alog_col, dtb_col, nw_row, oa, sga, sgb, x2d, woa, wob, wout, lng, lnb]
    has_state = s0 is not None
    if has_state:
        in_specs.append(state)
        args.append(s0)
    return pl.pallas_call(
        functools.partial(_delta_kernel, has_state=has_state, n_ch=n_ch),
        grid=(bsz, n_t),
        in_specs=in_specs,
        out_specs=[cur(D_MODEL), state],
        out_shape=[jax.ShapeDtypeStruct((n, D_MODEL), F32),
                   jax.ShapeDtypeStruct((bsz, N_HEADS_B, HEAD_DK_B, HEAD_DV_B), F32)],
        scratch_shapes=[pltpu.VMEM((n_ch * CHUNK, WIDTH_B), BF16)],
        compiler_params=pltpu.CompilerParams(dimension_semantics=("parallel", "arbitrary"),
                                             vmem_limit_bytes=VMEM_LIMIT),
        name="gated_delta_out",
    )(*args)


def _rope_tables(pos):
    half = ROT_DIM // 2
    inv_freq = jnp.power(ROPE_THETA, -jnp.arange(half, dtype=F32) / half)
    ang = pos[:, None] * inv_freq[None, :]
    cos, sin = jnp.cos(ang), jnp.sin(ang)
    n = pos.shape[0]
    ones = jnp.ones((n, HEAD_DIM_A - ROT_DIM), F32)
    zeros = jnp.zeros((n, HEAD_DIM_A - ROT_DIM), F32)
    zh = jnp.zeros((n, half), F32)
    ct = jnp.concatenate([cos, cos, ones], axis=1)
    sa = jnp.concatenate([zh, sin, zeros], axis=1)
    sb = jnp.concatenate([-sin, zh, zeros], axis=1)
    tile2 = lambda t: jnp.concatenate([t, t], axis=1)
    return tile2(ct), tile2(sa), tile2(sb)


def _layer(x, pos_offset, k_hist, v_hist, conv_hist, s0, weights):
    (w_r, sinks2, conv_w, alog_col, dtb_col, nw_row, woa, wob, wout, lng, lnb) = weights
    bsz, l_len, _ = x.shape
    n = bsz * l_len
    n_c = l_len // CHUNK
    tm = min(PROJ_TILE, l_len)
    n_ch = min(ATTN_TILE, l_len) // CHUNK
    x2d = x.reshape(n, D_MODEL)
    pos = jnp.arange(l_len, dtype=F32) + pos_offset
    ct, sa, sb = _rope_tables(pos)
    if k_hist is None:
        kh = vh = hist8 = None
    else:
        assert l_len == tm
        kh = k_hist.reshape(bsz * WINDOW, KV_WIDTH_A)
        vh = v_hist.reshape(bsz * WINDOW, KV_WIDTH_A)
        hist8 = jnp.pad(conv_hist, ((0, 0), (SUBLANES - (CONV_W - 1), 0), (0, 0)))
    q, k, v, gza, cs, gzb, sga, sgb, ab, tail = _projection(x2d, hist8, w_r, ct, sa, sb, conv_w, tm, l_len // tm)
    oa = _attention(sinks2, q, k, v, gza, kh, vh, bsz, n_c, n_ch)
    y, s_new = _delta_out(cs, ab, gzb, alog_col, dtb_col, nw_row, oa, sga, sgb, x2d, woa, wob, wout, lng, lnb, s0,
                          bsz, n_c, min(DELTA_TILE, l_len) // CHUNK)

    k3 = k.reshape(bsz, l_len, N_KV_A, HEAD_DIM_A)
    v3 = v.reshape(bsz, l_len, N_KV_A, HEAD_DIM_A)
    if k_hist is None:
        new_k, new_v = k3[:, l_len - WINDOW:], v3[:, l_len - WINDOW:]
    else:
        keep = k_hist.shape[1]
        new_k = jnp.concatenate([k_hist, k3], axis=1)[:, l_len:]
        new_v = jnp.concatenate([v_hist, v3], axis=1)[:, l_len:]
        assert new_k.shape[1] == keep == WINDOW
    new_conv = tail.reshape(bsz, l_len // tm, SUBLANES, CONV_CH)[:, -1, SUBLANES - (CONV_W - 1):]
    return y.reshape(bsz, l_len, D_MODEL), new_k, new_v, new_conv, s_new


def kernel(x_prompt, x_sample, cache_attn_k, cache_attn_v, state_conv, state_delta, w_in, attn_sinks, conv_w,
           a_log, dt_bias, delta_norm_w, w_o_attn, w_o_delta, w_out, ln_g, ln_b):
    splits = np.cumsum((WIDTH_A, KV_WIDTH_A, KV_WIDTH_A, WIDTH_A, CONV_CH, N_HEADS_B, N_HEADS_B, WIDTH_B,
                        D_MODEL, D_MODEL))
    ab_lo, ab_hi = int(splits[4]), int(splits[6])
    w_r = jnp.concatenate([w_in[:, :ab_lo], w_in[:, ab_hi:], w_in[:, ab_lo:ab_hi],
                           jnp.zeros((D_MODEL, AB_PAD - (ab_hi - ab_lo)), w_in.dtype)], axis=1).astype(BF16)
    head_col = lambda t: t.astype(F32).reshape(N_HEADS_B, 1)
    weights = (w_r, attn_sinks.astype(F32) * LOG2E, conv_w.astype(F32), head_col(a_log), head_col(dt_bias),
               delta_norm_w.astype(F32).reshape(1, HEAD_DV_B), w_o_attn.astype(BF16), w_o_delta.astype(BF16),
               w_out.astype(BF16), ln_g.astype(F32).reshape(1, D_MODEL), ln_b.astype(F32).reshape(1, D_MODEL))
    yp, kp, vp, cp, sp = _layer(x_prompt, 0.0, None, None, None, None, weights)
    ys, ks, vs, cs, ss = _layer(x_sample, float(PAST_LEN), cache_attn_k, cache_attn_v, state_conv, state_delta,
                                weights)
    return (yp, ys, kp, vp, cp, sp, ks, vs, cs, ss)
```

```python
import functools
import math

import numpy as np
import jax
import jax.numpy as jnp
from jax import lax
from jax.experimental import pallas as pl
from jax.experimental.pallas import tpu as pltpu

D_MODEL = 1024
CHUNK = 64
N_HEADS_A = 16
N_KV_A = 4
HEAD_DIM_A = 64
GROUP_A = N_HEADS_A // N_KV_A
WINDOW = 128
ROT_DIM = HEAD_DIM_A // 4
ROPE_THETA = 500000.0
WIDTH_A = N_HEADS_A * HEAD_DIM_A
KV_WIDTH_A = N_KV_A * HEAD_DIM_A
N_HEADS_B = 8
HEAD_DK_B = 128
HEAD_DV_B = 128
CONV_W = 4
QK_WIDTH_B = N_HEADS_B * HEAD_DK_B
WIDTH_B = N_HEADS_B * HEAD_DV_B
CONV_CH = 2 * QK_WIDTH_B + WIDTH_B
DEPTH = 1
PAST_LEN = 1024
ALPHA = (2.0 * DEPTH) ** 0.25
LN_EPS = 1e-5
RMS_EPS = 1e-6
L2_EPS = 1e-6
NEG_INF = -1e30
LOG2E = math.log2(math.e)

LANES = 128
SUBLANES = 8
AB_PAD = LANES
OFF_Q = 0
OFF_K = OFF_Q + WIDTH_A
OFF_V = OFF_K + KV_WIDTH_A
OFF_ZA = OFF_V + KV_WIDTH_A
OFF_QKVB = OFF_ZA + WIDTH_A
OFF_ZB = OFF_QKVB + CONV_CH
OFF_GA = OFF_ZB + WIDTH_B
OFF_GB = OFF_GA + D_MODEL
OFF_AB = OFF_GB + D_MODEL
D_IN_PAD = OFF_AB + AB_PAD

VMEM_LIMIT = 56 * 1024 * 1024
PROJ_SEG = 512
EPI_WEIGHT_CONV, EPI_WEIGHT_ROPE, EPI_WEIGHT_ACT = 11, 16, 9
PROJ_TILE = 512
ATTN_TILE = 256
ATTN_LAG = 2
DELTA_TILE = 512
DELTA_GROUP = 2
OUT_COLS = 256
OUT_GROUPS = 2

F32 = jnp.float32
BF16 = jnp.bfloat16


def _sigmoid(x):
    return 0.5 + 0.5 * jnp.tanh(0.5 * x)


def _silu(x):
    h = 0.5 * x
    return h + h * jnp.tanh(h)


def _softplus(x):
    return jnp.maximum(x, 0.0) + jnp.log1p(jnp.exp(-jnp.abs(x)))


def _dot(a, b):
    return jnp.dot(a.astype(BF16), b.astype(BF16), preferred_element_type=F32)


def _dot_nt(a, b):
    return lax.dot_general(a.astype(BF16), b.astype(BF16), (((1,), (1,)), ((), ())),
                           preferred_element_type=F32)


def _interleave(streams):
    tagged = [((i + 0.5) / len(s), si, i, t) for si, s in enumerate(streams) for i, t in enumerate(s)]
    tagged.sort(key=lambda x: x[:3])
    return [t for *_, t in tagged]


def _spread(mm, epi):
    total = sum(w for e in epi for w, _ in e)
    order, queue, emitted = [], [], 0
    for i, m in enumerate(mm):
        order.append(m)
        if i > 0:
            queue.extend(epi[i - 1])
        while queue and emitted < total * i / len(mm):
            w, thunk = queue.pop(0)
            order.append(thunk)
            emitted += w
    queue.extend(epi[-1])
    order.extend(thunk for _, thunk in queue)
    return order


def _lagged(streams, lag):
    n = len(streams[0])
    order = []
    for i in range(n + lag * (len(streams) - 1)):
        for j, s in enumerate(streams):
            if 0 <= i - j * lag < n:
                order.append(s[i - j * lag])
    return order


def _rope_cols(t, ct, sa, sb):
    return t * ct + pltpu.roll(t, 8, axis=1) * sa + pltpu.roll(t, LANES - 8, axis=1) * sb


def _proj_kernel(*refs, tm, tiles_per_seq, has_hist):
    if has_hist:
        (x_ref, hist_ref, w_ref, ct_ref, sa_ref, sb_ref, cw_ref,
         q_ref, k_ref, v_ref, gza_ref, cs_ref, gzb_ref, sga_ref, sgb_ref, ab_ref, tail_ref) = refs
        xb = x_ref[...].astype(BF16)
        xb_look = xb
    else:
        (x_ref, xprev_ref, w_ref, ct_ref, sa_ref, sb_ref, cw_ref,
         q_ref, k_ref, v_ref, gza_ref, cs_ref, gzb_ref, sga_ref, sgb_ref, ab_ref, tail_ref) = refs
        xb = x_ref[...].astype(BF16)
        xb_look = jnp.concatenate([xb, xprev_ref[...].astype(BF16)], axis=0)
        first_tile = pl.program_id(0) % tiles_per_seq == 0
    ct, sa, sb = ct_ref[...], sa_ref[...], sb_ref[...]
    res = {}

    def matmul(key, off, width, lhs):
        res[key] = jnp.dot(lhs, w_ref[:, off:off + width], preferred_element_type=F32)

    def slabs(width):
        return [slice(j * LANES, (j + 1) * LANES) for j in range(width // LANES)]

    def epi_q(key, off):
        r = res.pop(key)
        for sl in slabs(r.shape[1]):
            qj = _rope_cols(r[:, sl], ct, sa, sb) * (HEAD_DIM_A ** -0.5 * LOG2E)
            q_ref[:, off + sl.start:off + sl.stop] = qj.astype(q_ref.dtype)

    def epi_kv(key):
        r = res.pop(key)
        for sl in slabs(KV_WIDTH_A):
            k_ref[:, sl] = _rope_cols(r[:, sl], ct, sa, sb)
        v_ref[...] = r[:, KV_WIDTH_A:]

    def epi_act(key, fn, out_ref, off):
        r = res.pop(key)
        out_ref[:, off:off + r.shape[1]] = fn(r).astype(out_ref.dtype)

    def epi_conv(key, off, sl):
        r = res[key]
        cols = slice(off + sl.start, off + sl.stop)
        tail_ref[:, cols] = r[tm - SUBLANES:tm, sl]
        if has_hist:
            look = hist_ref[0, :, cols]
        else:
            look = jnp.where(first_tile, 0.0, r[tm:tm + SUBLANES, sl])
        cur = r[0:tm, sl]
        xp = jnp.concatenate([look, cur], axis=0)
        acc = cur * cw_ref[CONV_W - 1:CONV_W, cols]
        for j in range(1, CONV_W):
            acc = acc + pltpu.roll(xp, j, axis=0)[SUBLANES:] * cw_ref[CONV_W - 1 - j:CONV_W - j, cols]
        cs_ref[:, cols] = _silu(acc).astype(cs_ref.dtype)

    def epi_ab(key):
        ab_ref[...] = res.pop(key)

    mm, epi = [], []

    def add(key, off, width, epilogues, lhs=None):
        mm.append(functools.partial(matmul, key, off, width, xb if lhs is None else lhs))
        epi.append(epilogues)

    for j in range(CONV_CH // PROJ_SEG):
        add(("qkvb", j), OFF_QKVB + j * PROJ_SEG, PROJ_SEG,
            [(EPI_WEIGHT_CONV, functools.partial(epi_conv, ("qkvb", j), j * PROJ_SEG, sl))
             for sl in slabs(PROJ_SEG)], lhs=xb_look)
    for j in range(WIDTH_A // PROJ_SEG):
        add(("q", j), OFF_Q + j * PROJ_SEG, PROJ_SEG,
            [(EPI_WEIGHT_ROPE, functools.partial(epi_q, ("q", j), j * PROJ_SEG))])
    add("kv", OFF_K, 2 * KV_WIDTH_A, [(EPI_WEIGHT_ROPE // 2, functools.partial(epi_kv, "kv"))])
    for name, base, fn, out_ref in (("za", OFF_ZA, _silu, gza_ref), ("zb", OFF_ZB, _silu, gzb_ref),
                                    ("ga", OFF_GA, _sigmoid, sga_ref), ("gb", OFF_GB, _sigmoid, sgb_ref)):
        for j in range(D_MODEL // PROJ_SEG):
            add((name, j), base + j * PROJ_SEG, PROJ_SEG,
                [(EPI_WEIGHT_ACT, functools.partial(epi_act, (name, j), fn, out_ref, j * PROJ_SEG))])
    add("ab", OFF_AB, AB_PAD, [(1, functools.partial(epi_ab, "ab"))])
    for thunk in _spread(mm, epi):
        thunk()


def _projection(x2d, hist8, w_r, ct, sa, sb, conv_w, tm, tiles_per_seq):
    n = x2d.shape[0]
    n_tiles = n // tm
    t_rows = ct.shape[0]
    n_t = t_rows // tm
    has_hist = hist8 is not None
    row = lambda w: pl.BlockSpec((tm, w), lambda i: (i, 0))
    tab = pl.BlockSpec((tm, LANES), lambda i: (i % n_t, 0))
    if has_hist:
        assert tiles_per_seq == 1
        look_spec = pl.BlockSpec((1, SUBLANES, CONV_CH), lambda i: (i, 0, 0))
        look = hist8
    else:
        blocks = tm // SUBLANES
        look_spec = pl.BlockSpec((SUBLANES, D_MODEL), lambda i: (jnp.maximum(i * blocks - 1, 0), 0))
        look = x2d
    widths = (WIDTH_A, KV_WIDTH_A, KV_WIDTH_A, WIDTH_A, CONV_CH, WIDTH_B, D_MODEL, D_MODEL, AB_PAD)
    dtypes = (BF16, F32, F32, BF16, BF16, BF16, BF16, BF16, F32)
    return pl.pallas_call(
        functools.partial(_proj_kernel, tm=tm, tiles_per_seq=tiles_per_seq, has_hist=has_hist),
        grid=(n_tiles,),
        in_specs=[row(D_MODEL), look_spec,
                  pl.BlockSpec((D_MODEL, D_IN_PAD), lambda i: (0, 0), pipeline_mode=pl.Buffered(1)),
                  tab, tab, tab, pl.BlockSpec((CONV_W, CONV_CH), lambda i: (0, 0))],
        out_specs=[row(w) for w in widths] + [pl.BlockSpec((SUBLANES, CONV_CH), lambda i: (i, 0))],
        out_shape=[jax.ShapeDtypeStruct((n, w), d) for w, d in zip(widths, dtypes)]
        + [jax.ShapeDtypeStruct((n_tiles * SUBLANES, CONV_CH), F32)],
        compiler_params=pltpu.CompilerParams(dimension_semantics=("parallel",),
                                             vmem_limit_bytes=VMEM_LIMIT),
        name="in_projection",
    )(x2d, look, w_r, ct, sa, sb, conv_w)


def _attn_body(sinks_ref, q_ref, kprev_ref, kcur_ref, vprev_ref, vcur_ref, gate_ref, o_ref,
               *, n_ch, prev_valid):
    gs = 2 if n_ch % 2 == 0 else 1
    n_keys = (gs + 2) * CHUNK
    rows_u = gs * 2 * CHUNK
    kext = jnp.concatenate([kprev_ref[...], kcur_ref[...]], axis=0)
    vext = jnp.concatenate([vprev_ref[...], vcur_ref[...]], axis=0)
    lo = lax.broadcasted_iota(jnp.int32, (kext.shape[0], LANES), 1) < HEAD_DIM_A
    r_idx = lax.broadcasted_iota(jnp.int32, (rows_u, n_keys), 0)
    k_idx = lax.broadcasted_iota(jnp.int32, (rows_u, n_keys), 1)
    first_pair = (lax.broadcasted_iota(jnp.int32, (rows_u, 1), 0) // CHUNK) % 2 == 0
    ci = r_idx // (2 * CHUNK)
    band = (k_idx >= ci * CHUNK) & (k_idx < (ci + 3) * CHUNK)

    kv_ops = []
    for kv in range(N_KV_A):
        col, odd = kv // 2, kv % 2
        kc = kext[:, col * LANES:(col + 1) * LANES]
        vc = vext[:, col * LANES:(col + 1) * LANES]
        ks = pltpu.roll(kc, HEAD_DIM_A, axis=1)
        vs = pltpu.roll(vc, HEAD_DIM_A, axis=1)
        if odd:
            kc, ks, vc, vs = ks, kc, vs, vc
        kv_ops.append(((jnp.where(lo, kc, 0.0).astype(BF16), jnp.where(lo, vc, 0.0).astype(BF16)),
                       (jnp.where(lo, 0.0, ks).astype(BF16), jnp.where(lo, 0.0, vs).astype(BF16))))

    units = [(g, kv, half) for g in range(n_ch // gs) for kv in range(N_KV_A) for half in range(2)]
    st = {}

    def scores(g, kv, half):
        base = kv * GROUP_A * HEAD_DIM_A
        start = g * gs * CHUNK
        k_m = kv_ops[kv][half][0][start:start + n_keys]
        qst = jnp.concatenate(
            [q_ref[(g * gs + c) * CHUNK:(g * gs + c + 1) * CHUNK, base + pair * LANES:base + (pair + 1) * LANES]
             for c in range(gs) for pair in range(2)], axis=0)
        st["s", g, kv, half] = _dot_nt(qst, k_m)

    def softmax(g, kv, half):
        start = g * gs * CHUNK
        s = st.pop(("s", g, kv, half))
        ok = None
        if gs > 1:
            ok = band
        if not prev_valid and start < WINDOW:
            valid = k_idx + start >= WINDOW
            ok = valid if ok is None else ok & valid
        if ok is not None:
            s = jnp.where(ok, s, NEG_INF)
        sink = jnp.where(first_pair, sinks_ref[kv * GROUP_A + half], sinks_ref[kv * GROUP_A + 2 + half])
        m = jnp.maximum(jnp.max(s, axis=-1, keepdims=True), sink)
        p = jnp.exp2(s - m)
        den = jnp.sum(p, axis=-1, keepdims=True) + jnp.exp2(sink - m)
        st[g, kv, half] = (p.astype(BF16), 1.0 / den)

    def values(g, kv, half):
        base = kv * GROUP_A * HEAD_DIM_A
        start = g * gs * CHUNK
        p, rden = st.pop((g, kv, half))
        v_m = kv_ops[kv][half][1][start:start + n_keys]
        o = jnp.dot(p, v_m, preferred_element_type=F32) * rden
        if half == 0:
            st[g, kv] = o
            return
        acc = st.pop((g, kv)) + o
        for c in range(gs):
            rows = slice((g * gs + c) * CHUNK, (g * gs + c + 1) * CHUNK)
            for pair in range(2):
                sl = slice(base + pair * LANES, base + (pair + 1) * LANES)
                blk = acc[(2 * c + pair) * CHUNK:(2 * c + pair + 1) * CHUNK]
                o_ref[rows, sl] = (blk * gate_ref[rows, sl].astype(F32)).astype(o_ref.dtype)

    for thunk in _lagged([[functools.partial(fn, *u) for u in units] for fn in (scores, softmax, values)],
                         ATTN_LAG):
        thunk()


def _attn_kernel(*refs, n_ch, has_hist):
    if has_hist:
        _attn_body(*refs, n_ch=n_ch, prev_valid=True)
        return
    t = pl.program_id(1)

    @pl.when(t == 0)
    def _():
        _attn_body(*refs, n_ch=n_ch, prev_valid=False)

    @pl.when(t > 0)
    def _():
        _attn_body(*refs, n_ch=n_ch, prev_valid=True)


def _attention(sinks2, q, k, v, gate, k_hist, v_hist, bsz, n_c, n_ch):
    n = q.shape[0]
    n_t = n_c // n_ch
    tok = n_ch * CHUNK
    cur = lambda w: pl.BlockSpec((tok, w), lambda b, t: (b * n_t + t, 0))
    if k_hist is None:
        per_seq = n_c * CHUNK // WINDOW
        prev = pl.BlockSpec((WINDOW, KV_WIDTH_A),
                            lambda b, t: (b * per_seq + jnp.maximum(t * (tok // WINDOW) - 1, 0), 0))
        kp, vp = k, v
    else:
        assert n_t == 1
        prev = pl.BlockSpec((WINDOW, KV_WIDTH_A), lambda b, t: (b, 0))
        kp, vp = k_hist, v_hist
    return pl.pallas_call(
        functools.partial(_attn_kernel, n_ch=n_ch, has_hist=k_hist is not None),
        grid=(bsz, n_t),
        in_specs=[pl.BlockSpec(memory_space=pltpu.SMEM),
                  cur(WIDTH_A), prev, cur(KV_WIDTH_A), prev, cur(KV_WIDTH_A), cur(WIDTH_A)],
        out_specs=cur(WIDTH_A),
        out_shape=jax.ShapeDtypeStruct((n, WIDTH_A), BF16),
        compiler_params=pltpu.CompilerParams(dimension_semantics=("parallel", "parallel"),
                                             vmem_limit_bytes=VMEM_LIMIT),
        name="band_attention",
    )(sinks2, q, kp, k, vp, v, gate)


def _lane_col(x, lane_idx, j):
    return jnp.sum(jnp.where(lane_idx == j, x, 0.0), axis=-1, keepdims=True)


def _delta_kernel(*refs, has_state, n_ch):
    (cs_ref, ab_ref, gate_ref, alog_ref, dtb_ref, nw_ref,
     oa_ref, sga_ref, sgb_ref, x_ref, woa_ref, wob_ref, wout_ref, lng_ref, lnb_ref) = refs[:15]
    if has_state:
        s0_ref, y_ref, s_ref, ob_ref = refs[15:]
    else:
        y_ref, s_ref, ob_ref = refs[15:]

    @pl.when(pl.program_id(1) == 0)
    def _():
        if has_state:
            s_ref[...] = s0_ref[...]
        else:
            s_ref[...] = jnp.zeros_like(s_ref)

    ri = lax.broadcasted_iota(jnp.int32, (CHUNK, LANES), 0)
    li = lax.broadcasted_iota(jnp.int32, (CHUNK, LANES), 1)
    ci = li % CHUNK
    lo = li < CHUNK
    lo_row = lax.broadcasted_iota(jnp.int32, (1, LANES), 1) < CHUNK
    causal = ri >= ci
    strict = ri > ci
    eye2 = jnp.where(ri == ci, 1.0, 0.0).astype(F32)
    si = lax.broadcasted_iota(jnp.int32, (LANES, LANES), 0)
    sl = lax.broadcasted_iota(jnp.int32, (LANES, LANES), 1)
    cum2 = jnp.where((si < CHUNK) & (si <= sl % CHUNK), 1.0, 0.0).astype(F32)
    last_lane = lax.broadcasted_iota(jnp.int32, (N_HEADS_B, LANES), 1) == CHUNK - 1
    heads = range(N_HEADS_B)
    pairs = range(N_HEADS_B // 2)
    zero_rhs = jnp.zeros((CHUNK, 2 * HEAD_DV_B), BF16)
    zero_v = jnp.zeros((CHUNK, HEAD_DV_B), BF16)
    zero_k = jnp.zeros((CHUNK, HEAD_DK_B), BF16)
    ops = {(c, p): {} for c in range(n_ch) for p in pairs}
    gates = [None] * n_ch

    def blockdiag(x):
        return jnp.concatenate([jnp.where(lo, x, 0.0), jnp.where(lo, 0.0, x)], axis=0).astype(BF16)

    def gate_stage(c):
        ab = ab_ref[c * CHUNK:(c + 1) * CHUNK, :]
        ab_t = jnp.concatenate([ab, ab], axis=0).T
        g_t = -jnp.exp(alog_ref[...]) * _softplus(ab_t[0:N_HEADS_B] + dtb_ref[...])
        gc_t = jnp.dot(g_t, cum2, preferred_element_type=F32, precision=lax.Precision.HIGHEST)
        gl = jnp.sum(jnp.where(last_lane, gc_t, 0.0), axis=-1, keepdims=True)
        pad = jnp.zeros((LANES - 3 * N_HEADS_B, LANES), F32)
        packed = jnp.concatenate([gc_t, jnp.exp(gc_t), -_sigmoid(ab_t[N_HEADS_B:2 * N_HEADS_B]), pad],
                                 axis=0)
        gates[c] = dict(gc_t=gc_t, cols=packed.T[:CHUNK], e_rev_t=jnp.exp(gl - gc_t),
                        e_gl=jnp.broadcast_to(jnp.exp(gl), (N_HEADS_B, LANES)))

    def prep(c, p):
        gt, op = gates[c], ops[c, p]
        rows = slice(c * CHUNK, (c + 1) * CHUNK)
        slab = lambda j: cs_ref[rows, j * LANES:(j + 1) * LANES].astype(F32)
        per = []
        for h in (2 * p, 2 * p + 1):
            col = lambda i: _lane_col(gt["cols"], li, N_HEADS_B * i + h)
            q, k, v = slab(h), slab(N_HEADS_B + h), slab(2 * N_HEADS_B + h)
            q = q * (lax.rsqrt(jnp.sum(q * q, axis=-1, keepdims=True) + L2_EPS) * (HEAD_DK_B ** -0.5))
            k = k * lax.rsqrt(jnp.sum(k * k, axis=-1, keepdims=True) + L2_EPS)
            gc_col, e_gc, nb = col(0), col(1), col(2)
            kbn = k * nb
            per.append(dict(
                lhs=jnp.concatenate([kbn.astype(BF16), q.astype(BF16)], axis=0),
                k=k.astype(BF16),
                rhs_uw=jnp.concatenate([(v * -nb).astype(BF16), (kbn * e_gc).astype(BF16)], axis=1),
                q_dec=(q * e_gc).astype(BF16), k_f32=k, gc_col=gc_col,
                e_gl=gt["e_gl"][h:h + 1, :]))
        a, b = per
        pair_row = lambda t: jnp.where(lo_row, t[2 * p:2 * p + 1, :], t[2 * p + 1:2 * p + 2, :])
        gc_row = pair_row(gt["gc_t"])
        diff = jnp.where(lo, a["gc_col"], b["gc_col"]) - gc_row
        op.update(
            lhs=jnp.concatenate([a["lhs"], b["lhs"]], axis=1),
            k_bd=jnp.concatenate([jnp.concatenate([a["k"], zero_k], axis=1),
                                  jnp.concatenate([zero_k, b["k"]], axis=1)], axis=0),
            rhs_uw=(jnp.concatenate([a["rhs_uw"], zero_rhs], axis=0),
                    jnp.concatenate([zero_rhs, b["rhs_uw"]], axis=0)),
            q_dec=(a["q_dec"], b["q_dec"]),
            k_dec_t=(jnp.concatenate([a["k_f32"], b["k_f32"]], axis=0).T * pair_row(gt["e_rev_t"])).astype(BF16),
            decay=jnp.where(causal, jnp.exp(jnp.where(causal, diff, 0.0)), 0.0),
            e_gl=(a["e_gl"], b["e_gl"]))

    def scores(c, p):
        op = ops[c, p]
        aq = _dot_nt(op["lhs"], op["k_bd"])
        op["p"] = jnp.where(strict, aq[:CHUNK] * op["decay"], 0.0)
        op["qk"] = (aq[CHUNK:] * op["decay"]).astype(BF16)

    def inv_first(c, p):
        op = ops[c, p]
        op["t"] = eye2 + op["p"]
        op["p"] = jnp.dot(op["p"].astype(BF16), blockdiag(op["p"]), preferred_element_type=F32)

    def inv_mid(c, p):
        op = ops[c, p]
        r = jnp.dot(jnp.concatenate([op["p"].astype(BF16), op["t"].astype(BF16)], axis=0), blockdiag(op["p"]),
                    preferred_element_type=F32)
        op["p"] = r[:CHUNK]
        op["t"] = op["t"] + r[CHUNK:]

    def inv_last(c, p):
        op = ops[c, p]
        op["t"] = op["t"] + jnp.dot(op["t"].astype(BF16), blockdiag(op["p"]), preferred_element_type=F32)

    def apply_inv(c, p):
        op = ops[c, p]
        tb = op["t"].astype(BF16)
        op["u"], op["wq"] = [], []
        for j in range(2):
            uw = jnp.dot(tb, op["rhs_uw"][j], preferred_element_type=F32)
            op["u"].append(uw[:, :HEAD_DV_B])
            op["wq"].append(jnp.concatenate([uw[:, HEAD_DV_B:].astype(BF16), op["q_dec"][j]], axis=0))

    state = [None] * N_HEADS_B

    def load_state(h):
        state[h] = s_ref[0, h]

    def rec_first(c, h):
        op, j = ops[c, h // 2], h % 2
        ws = _dot(op["wq"][j], state[h])
        op["v_new", j] = (op["u"][j] + ws[:CHUNK]).astype(BF16)
        op["qs", j] = ws[CHUNK:]

    def rec_second(c, p):
        op = ops[c, p]
        rows = slice(c * CHUNK, (c + 1) * CHUNK)
        v_bd = jnp.concatenate([jnp.concatenate([op["v_new", 0], zero_v], axis=1),
                                jnp.concatenate([zero_v, op["v_new", 1]], axis=1)], axis=0)
        r = jnp.dot(jnp.concatenate([op["qk"], op["k_dec_t"]], axis=0), v_bd,
                    preferred_element_type=F32)
        for j in range(2):
            h = 2 * p + j
            hs = slice(h * HEAD_DV_B, (h + 1) * HEAD_DV_B)
            rj = r[:, j * HEAD_DV_B:(j + 1) * HEAD_DV_B]
            o = op["qs", j] + rj[:CHUNK]
            state[h] = state[h] * op["e_gl"][j] + rj[CHUNK:]
            o = o * lax.rsqrt(jnp.mean(o * o, axis=-1, keepdims=True) + RMS_EPS) * nw_ref[...]
            ob_ref[rows, hs] = (o * gate_ref[rows, hs].astype(F32)).astype(ob_ref.dtype)

    col_blocks = [slice(j * OUT_COLS, (j + 1) * OUT_COLS) for j in range(D_MODEL // OUT_COLS)]
    outs = {}

    def mix(chunks, cb):
        rows = slice(chunks[0] * CHUNK, (chunks[-1] + 1) * CHUNK)
        y_a = jnp.dot(oa_ref[rows, :], woa_ref[:, cb], preferred_element_type=F32)
        y_b = jnp.dot(ob_ref[rows, :], wob_ref[:, cb], preferred_element_type=F32)
        hmix = sga_ref[rows, cb].astype(F32) * y_a + sgb_ref[rows, cb].astype(F32) * y_b
        outs.setdefault(chunks[0], []).append(hmix.astype(BF16))

    def join(chunks):
        outs[chunks[0]] = jnp.concatenate(outs[chunks[0]], axis=1)

    def project(chunks, cb):
        outs.setdefault(("sub", chunks[0]), []).append(
            jnp.dot(outs[chunks[0]], wout_ref[:, cb], preferred_element_type=F32))

    def norm(chunks):
        rows = slice(chunks[0] * CHUNK, (chunks[-1] + 1) * CHUNK)
        del outs[chunks[0]]
        r = ALPHA * x_ref[rows, :] + jnp.concatenate(outs.pop(("sub", chunks[0])), axis=1)
        mu = jnp.mean(r, axis=-1, keepdims=True)
        d = r - mu
        var = jnp.mean(d * d, axis=-1, keepdims=True)
        y_ref[rows, :] = d * lax.rsqrt(var + LN_EPS) * lng_ref[...] + lnb_ref[...]

    def output_stream(chunks):
        span = OUT_GROUPS * len(chunks)
        if (chunks[-1] + 1) % span and chunks[-1] + 1 != n_ch:
            return []
        first = chunks[-1] + 1 - span if (chunks[-1] + 1) % span == 0 else (chunks[-1] // span) * span
        chunks = list(range(first, chunks[-1] + 1))
        return ([functools.partial(mix, chunks, cb) for cb in col_blocks] + [functools.partial(join, chunks)]
                + [functools.partial(project, chunks, cb) for cb in col_blocks] + [functools.partial(norm, chunks)])

    def over(fn, chunks, idx):
        return [functools.partial(fn, c, i) for c in chunks for i in idx]

    def prep_stream(chunks):
        return [functools.partial(gate_stage, c) for c in chunks] + over(prep, chunks, pairs)

    def level_stream(levels):
        return lambda chunks: [t for fn in levels for t in over(fn, chunks, pairs)]

    def recurrence_stream(chunks):
        return [t for c in chunks for t in over(rec_first, [c], heads) + over(rec_second, [c], pairs)]

    stages = [prep_stream,
              level_stream([scores, inv_first, inv_mid]),
              level_stream([inv_mid, inv_mid, inv_mid]),
              level_stream([inv_last, apply_inv]),
              recurrence_stream,
              output_stream]
    group = DELTA_GROUP if n_ch % DELTA_GROUP == 0 else 1
    groups = [list(range(g * group, (g + 1) * group)) for g in range(n_ch // group)]
    for h in heads:
        load_state(h)
    for step in range(len(groups) + len(stages) - 1):
        streams = [stage(groups[step - k]) for k, stage in reversed(list(enumerate(stages)))
                   if 0 <= step - k < len(groups)]
        for thunk in _interleave(streams):
            thunk()
    for h in heads:
        s_ref[0, h] = state[h]


def _delta_out(cs, ab, gate, alog_col, dtb_col, nw_row, oa, sga, sgb, x2d, woa, wob, wout, lng, lnb, s0,
               bsz, n_c, n_ch):
    n = cs.shape[0]
    n_t = n_c // n_ch
    cur = lambda w: pl.BlockSpec((n_ch * CHUNK, w), lambda b, t: (b * n_t + t, 0))
    const = lambda shape: pl.BlockSpec(shape, lambda b, t: (0,) * len(shape))
    weight = pl.BlockSpec((D_MODEL, D_MODEL), lambda b, t: (0, 0), pipeline_mode=pl.Buffered(1))
    state = pl.BlockSpec((1, N_HEADS_B, HEAD_DK_B, HEAD_DV_B), lambda b, t: (b, 0, 0, 0))
    in_specs = [cur(CONV_CH), cur(AB_PAD), cur(WIDTH_B),
                const((N_HEADS_B, 1)), const((N_HEADS_B, 1)), const((1, HEAD_DV_B)),
                cur(WIDTH_A), cur(D_MODEL), cur(D_MODEL), cur(D_MODEL), weight, weight, weight,
                const((1, D_MODEL)), const((1, D_MODEL))]
    args = [cs, ab, gate, alog_col, dtb_col, nw_row, oa, sga, sgb, x2d, woa, wob, wout, lng, lnb]
    has_state = s0 is not None
    if has_state:
        in_specs.append(state)
        args.append(s0)
    return pl.pallas_call(
        functools.partial(_delta_kernel, has_state=has_state, n_ch=n_ch),
        grid=(bsz, n_t),
        in_specs=in_specs,
        out_specs=[cur(D_MODEL), state],
        out_shape=[jax.ShapeDtypeStruct((n, D_MODEL), F32),
                   jax.ShapeDtypeStruct((bsz, N_HEADS_B, HEAD_DK_B, HEAD_DV_B), F32)],
        scratch_shapes=[pltpu.VMEM((n_ch * CHUNK, WIDTH_B), BF16)],
        compiler_params=pltpu.CompilerParams(dimension_semantics=("parallel", "arbitrary"),
                                             vmem_limit_bytes=VMEM_LIMIT),
        name="gated_delta_out",
    )(*args)


def _rope_tables(pos):
    half = ROT_DIM // 2
    inv_freq = jnp.power(ROPE_THETA, -jnp.arange(half, dtype=F32) / half)
    ang = pos[:, None] * inv_freq[None, :]
    cos, sin = jnp.cos(ang), jnp.sin(ang)
    n = pos.shape[0]
    ones = jnp.ones((n, HEAD_DIM_A - ROT_DIM), F32)
    zeros = jnp.zeros((n, HEAD_DIM_A - ROT_DIM), F32)
    zh = jnp.zeros((n, half), F32)
    ct = jnp.concatenate([cos, cos, ones], axis=1)
    sa = jnp.concatenate([zh, sin, zeros], axis=1)
    sb = jnp.concatenate([-sin, zh, zeros], axis=1)
    tile2 = lambda t: jnp.concatenate([t, t], axis=1)
    return tile2(ct), tile2(sa), tile2(sb)


def _layer(x, pos_offset, k_hist, v_hist, conv_hist, s0, weights):
    (w_r, sinks2, conv_w, alog_col, dtb_col, nw_row, woa, wob, wout, lng, lnb) = weights
    bsz, l_len, _ = x.shape
    n = bsz * l_len
    n_c = l_len // CHUNK
    tm = min(PROJ_TILE, l_len)
    n_ch = min(ATTN_TILE, l_len) // CHUNK
    x2d = x.reshape(n, D_MODEL)
    pos = jnp.arange(l_len, dtype=F32) + pos_offset
    ct, sa, sb = _rope_tables(pos)
    if k_hist is None:
        kh = vh = hist8 = None
    else:
        assert l_len == tm
        kh = k_hist.reshape(bsz * WINDOW, KV_WIDTH_A)
        vh = v_hist.reshape(bsz * WINDOW, KV_WIDTH_A)
        hist8 = jnp.pad(conv_hist, ((0, 0), (SUBLANES - (CONV_W - 1), 0), (0, 0)))
    q, k, v, gza, cs, gzb, sga, sgb, ab, tail = _projection(x2d, hist8, w_r, ct, sa, sb, conv_w, tm, l_len // tm)
    oa = _attention(sinks2, q, k, v, gza, kh, vh, bsz, n_c, n_ch)
    y, s_new = _delta_out(cs, ab, gzb, alog_col, dtb_col, nw_row, oa, sga, sgb, x2d, woa, wob, wout, lng, lnb, s0,
                          bsz, n_c, min(DELTA_TILE, l_len) // CHUNK)

    def split_heads(t):
        return jnp.stack([t[..., j * HEAD_DIM_A:(j + 1) * HEAD_DIM_A] for j in range(N_KV_A)], axis=2)

    k3 = k.reshape(bsz, l_len, KV_WIDTH_A)
    v3 = v.reshape(bsz, l_len, KV_WIDTH_A)
    if k_hist is None:
        new_k, new_v = split_heads(k3[:, l_len - WINDOW:]), split_heads(v3[:, l_len - WINDOW:])
    else:
        keep = k_hist.shape[1]
        new_k = jnp.concatenate([k_hist[:, l_len:], split_heads(k3)], axis=1)
        new_v = jnp.concatenate([v_hist[:, l_len:], split_heads(v3)], axis=1)
        assert new_k.shape[1] == keep == WINDOW
    new_conv = tail.reshape(bsz, l_len // tm, SUBLANES, CONV_CH)[:, -1, SUBLANES - (CONV_W - 1):]
    return y.reshape(bsz, l_len, D_MODEL), new_k, new_v, new_conv, s_new


def kernel(x_prompt, x_sample, cache_attn_k, cache_attn_v, state_conv, state_delta, w_in, attn_sinks, conv_w,
           a_log, dt_bias, delta_norm_w, w_o_attn, w_o_delta, w_out, ln_g, ln_b):
    splits = np.cumsum((WIDTH_A, KV_WIDTH_A, KV_WIDTH_A, WIDTH_A, CONV_CH, N_HEADS_B, N_HEADS_B, WIDTH_B,
                        D_MODEL, D_MODEL))
    ab_lo, ab_hi = int(splits[4]), int(splits[6])
    w_r = jnp.concatenate([w_in[:, :ab_lo], w_in[:, ab_hi:], w_in[:, ab_lo:ab_hi],
                           jnp.zeros((D_MODEL, AB_PAD - (ab_hi - ab_lo)), w_in.dtype)], axis=1).astype(BF16)
    head_col = lambda t: t.astype(F32).reshape(N_HEADS_B, 1)
    weights = (w_r, attn_sinks.astype(F32) * LOG2E, conv_w.astype(F32), head_col(a_log), head_col(dt_bias),
               delta_norm_w.astype(F32).reshape(1, HEAD_DV_B), w_o_attn.astype(BF16), w_o_delta.astype(BF16),
               w_out.astype(BF16), ln_g.astype(F32).reshape(1, D_MODEL), ln_b.astype(F32).reshape(1, D_MODEL))
    yp, kp, vp, cp, sp = _layer(x_prompt, 0.0, None, None, None, None, weights)
    ys, ks, vs, cs, ss = _layer(x_sample, float(PAST_LEN), cache_attn_k, cache_attn_v, state_conv, state_delta,
                                weights)
    return (yp, ys, kp, vp, cp, sp, ks, vs, cs, ss)
```

```python
import functools
import math

import numpy as np
import jax
import jax.numpy as jnp
from jax import lax
from jax.experimental import pallas as pl
from jax.experimental.pallas import tpu as pltpu

D_MODEL = 1024
CHUNK = 64
N_HEADS_A = 16
N_KV_A = 4
HEAD_DIM_A = 64
GROUP_A = N_HEADS_A // N_KV_A
WINDOW = 128
ROT_DIM = HEAD_DIM_A // 4
ROPE_THETA = 500000.0
WIDTH_A = N_HEADS_A * HEAD_DIM_A
KV_WIDTH_A = N_KV_A * HEAD_DIM_A
N_HEADS_B = 8
HEAD_DK_B = 128
HEAD_DV_B = 128
CONV_W = 4
QK_WIDTH_B = N_HEADS_B * HEAD_DK_B
WIDTH_B = N_HEADS_B * HEAD_DV_B
CONV_CH = 2 * QK_WIDTH_B + WIDTH_B
DEPTH = 1
PAST_LEN = 1024
ALPHA = (2.0 * DEPTH) ** 0.25
LN_EPS = 1e-5
RMS_EPS = 1e-6
L2_EPS = 1e-6
NEG_INF = -1e30
LOG2E = math.log2(math.e)

LANES = 128
SUBLANES = 8
AB_PAD = LANES
OFF_Q = 0
OFF_K = OFF_Q + WIDTH_A
OFF_V = OFF_K + KV_WIDTH_A
OFF_ZA = OFF_V + KV_WIDTH_A
OFF_QKVB = OFF_ZA + WIDTH_A
OFF_ZB = OFF_QKVB + CONV_CH
OFF_GA = OFF_ZB + WIDTH_B
OFF_GB = OFF_GA + D_MODEL
OFF_AB = OFF_GB + D_MODEL
D_IN_PAD = OFF_AB + AB_PAD

VMEM_LIMIT = 56 * 1024 * 1024
PROJ_SEG = 512
PROJ_TILE = 512
ATTN_TILE = 256
ATTN_LAG = 2
DELTA_TILE = 512
DELTA_GROUP = 2
OUT_COLS = 256
OUT_GROUPS = 2

F32 = jnp.float32
BF16 = jnp.bfloat16


def _sigmoid(x):
    return 0.5 + 0.5 * jnp.tanh(0.5 * x)


def _silu(x):
    h = 0.5 * x
    return h + h * jnp.tanh(h)


def _softplus(x):
    return jnp.maximum(x, 0.0) + jnp.log1p(jnp.exp(-jnp.abs(x)))


def _dot(a, b):
    return jnp.dot(a.astype(BF16), b.astype(BF16), preferred_element_type=F32)


def _dot_nt(a, b):
    return lax.dot_general(a.astype(BF16), b.astype(BF16), (((1,), (1,)), ((), ())),
                           preferred_element_type=F32)


def _interleave(streams):
    tagged = [((i + 0.5) / len(s), si, i, t) for si, s in enumerate(streams) for i, t in enumerate(s)]
    tagged.sort(key=lambda x: x[:3])
    return [t for *_, t in tagged]


def _lagged(streams, lag):
    n = len(streams[0])
    order = []
    for i in range(n + lag * (len(streams) - 1)):
        for j, s in enumerate(streams):
            if 0 <= i - j * lag < n:
                order.append(s[i - j * lag])
    return order


ROPE_TILE_RUNS = ((0, 8), (16, 40), (64, 72), (80, 104), (8, 16), (40, 64), (72, 80), (104, 128))
ROPE_TILE_RUNS_INV = ((0, 8), (64, 72), (8, 32), (72, 96), (32, 40), (96, 104), (40, 64), (104, 128))
HALF_TILE = LANES // 2
QUARTER_TILE = LANES // 4


def _permute_tiles(t, runs):
    return jnp.concatenate([t[..., base + a:base + b] for base in range(0, t.shape[-1], LANES) for a, b in runs],
                           axis=-1)


def _rope_cols(t, ct, sp):
    return t * ct + pltpu.roll(t, HALF_TILE, axis=1) * sp


def _proj_kernel(*refs, tm, piece, tiles_per_seq, has_hist):
    if has_hist:
        (x_ref, hist_ref, w_ref, ct_ref, sp_ref, cw_ref,
         q_ref, k_ref, v_ref, gza_ref, cs_ref, gzb_ref, sga_ref, sgb_ref, ab_ref, tail_ref) = refs
        xb = x_ref[...].astype(BF16)
        xb_look = xb
    else:
        (x_ref, xprev_ref, w_ref, ct_ref, sp_ref, cw_ref,
         q_ref, k_ref, v_ref, gza_ref, cs_ref, gzb_ref, sga_ref, sgb_ref, ab_ref, tail_ref) = refs
        xb = x_ref[...].astype(BF16)
        xb_look = jnp.concatenate([xb, xprev_ref[...].astype(BF16)], axis=0)
        first_tile = pl.program_id(0) % tiles_per_seq == 0
    ct, sp = ct_ref[...], sp_ref[...]
    res = {}

    def matmul(key, off, width, lhs):
        res[key] = jnp.dot(lhs, w_ref[:, off:off + width], preferred_element_type=F32)

    def slabs(width):
        return [slice(j * LANES, (j + 1) * LANES) for j in range(width // LANES)]

    def epi_q(key, off):
        r = res.pop(key)
        for sl in slabs(r.shape[1]):
            qj = _rope_cols(r[:, sl], ct, sp) * (HEAD_DIM_A ** -0.5 * LOG2E)
            q_ref[:, off + sl.start:off + sl.stop] = qj.astype(q_ref.dtype)

    def epi_kv(key):
        r = res.pop(key)
        for sl in slabs(KV_WIDTH_A):
            k_ref[:, sl] = _rope_cols(r[:, sl], ct, sp)
        v_ref[...] = r[:, KV_WIDTH_A:]

    def epi_act(key, fn, out_ref, off):
        r = res.pop(key)
        out_ref[:, off:off + r.shape[1]] = fn(r).astype(out_ref.dtype)

    def epi_conv(key, off, sl):
        r = res[key]
        cols = slice(off + sl.start, off + sl.stop)
        for i in range(tm // piece):
            if has_hist:
                look = hist_ref[i, :, cols]
            else:
                look = jnp.where(first_tile, 0.0, r[tm:tm + SUBLANES, sl])
            rows = slice(i * piece, (i + 1) * piece)
            cur = r[rows, sl]
            tail_ref[i * SUBLANES:(i + 1) * SUBLANES, cols] = cur[piece - SUBLANES:]
            xp = jnp.concatenate([look, cur], axis=0)
            acc = cur * cw_ref[CONV_W - 1:CONV_W, cols]
            for j in range(1, CONV_W):
                acc = acc + pltpu.roll(xp, j, axis=0)[SUBLANES:] * cw_ref[CONV_W - 1 - j:CONV_W - j, cols]
            cs_ref[rows, cols] = _silu(acc).astype(cs_ref.dtype)

    def epi_ab(key):
        ab_ref[...] = res.pop(key)

    def step(key, off, width, epilogues, lhs=None):
        return functools.partial(matmul, key, off, width, xb if lhs is None else lhs), epilogues

    heavy = [step(("qkvb", j), OFF_QKVB + j * PROJ_SEG, PROJ_SEG,
                  [functools.partial(epi_conv, ("qkvb", j), j * PROJ_SEG, sl) for sl in slabs(PROJ_SEG)], lhs=xb_look)
             for j in range(CONV_CH // PROJ_SEG)]
    light = [step(("q", j), OFF_Q + j * PROJ_SEG, PROJ_SEG, [functools.partial(epi_q, ("q", j), j * PROJ_SEG)])
             for j in range(WIDTH_A // PROJ_SEG)]
    light.append(step("kv", OFF_K, 2 * KV_WIDTH_A, [functools.partial(epi_kv, "kv")]))
    for name, base, fn, out_ref in (("za", OFF_ZA, _silu, gza_ref), ("zb", OFF_ZB, _silu, gzb_ref),
                                    ("ga", OFF_GA, _sigmoid, sga_ref), ("gb", OFF_GB, _sigmoid, sgb_ref)):
        light += [step((name, j), base + j * PROJ_SEG, PROJ_SEG,
                       [functools.partial(epi_act, (name, j), fn, out_ref, j * PROJ_SEG)])
                  for j in range(D_MODEL // PROJ_SEG)]
    light.append(step("ab", OFF_AB, AB_PAD, [functools.partial(epi_ab, "ab")]))
    pending = []
    for mm, epilogues in _interleave([heavy, light]):
        mm()
        for thunk in pending:
            thunk()
        pending = epilogues
    for thunk in pending:
        thunk()


def _projection(x2d, hist8, w_r, ct, sp, conv_w, tm, seq_len):
    n = x2d.shape[0]
    n_tiles = n // tm
    n_t = ct.shape[0] // tm
    has_hist = hist8 is not None
    row = lambda w: pl.BlockSpec((tm, w), lambda i: (i, 0))
    tab = pl.BlockSpec((tm, LANES), lambda i: (i % n_t, 0))
    if has_hist:
        piece, tiles_per_seq = seq_len, 1
        look_spec = pl.BlockSpec((tm // piece, SUBLANES, CONV_CH), lambda i: (i, 0, 0))
        look = hist8
    else:
        piece, tiles_per_seq = tm, seq_len // tm
        blocks = tm // SUBLANES
        look_spec = pl.BlockSpec((SUBLANES, D_MODEL), lambda i: (jnp.maximum(i * blocks - 1, 0), 0))
        look = x2d
    tail_rows = tm // piece * SUBLANES
    widths = (WIDTH_A, KV_WIDTH_A, KV_WIDTH_A, WIDTH_A, CONV_CH, WIDTH_B, D_MODEL, D_MODEL, AB_PAD)
    dtypes = (BF16, F32, F32, BF16, BF16, BF16, BF16, BF16, F32)
    return pl.pallas_call(
        functools.partial(_proj_kernel, tm=tm, piece=piece, tiles_per_seq=tiles_per_seq, has_hist=has_hist),
        grid=(n_tiles,),
        in_specs=[row(D_MODEL), look_spec,
                  pl.BlockSpec((D_MODEL, D_IN_PAD), lambda i: (0, 0), pipeline_mode=pl.Buffered(1)),
                  tab, tab, pl.BlockSpec((CONV_W, CONV_CH), lambda i: (0, 0))],
        out_specs=[row(w) for w in widths] + [pl.BlockSpec((tail_rows, CONV_CH), lambda i: (i, 0))],
        out_shape=[jax.ShapeDtypeStruct((n, w), d) for w, d in zip(widths, dtypes)]
        + [jax.ShapeDtypeStruct((n_tiles * tail_rows, CONV_CH), F32)],
        compiler_params=pltpu.CompilerParams(dimension_semantics=("parallel",),
                                             vmem_limit_bytes=VMEM_LIMIT),
        name="in_projection",
    )(x2d, look, w_r, ct, sp, conv_w)


def _attn_body(sinks_ref, q_ref, kprev_ref, kcur_ref, vprev_ref, vcur_ref, gate_ref, o_ref,
               *, n_ch, prev_valid):
    gs = 2 if n_ch % 2 == 0 else 1
    n_keys = (gs + 2) * CHUNK
    rows_u = gs * 2 * CHUNK
    kext = jnp.concatenate([kprev_ref[...], kcur_ref[...]], axis=0)
    vext = jnp.concatenate([vprev_ref[...], vcur_ref[...]], axis=0)
    key_lane = lax.broadcasted_iota(jnp.int32, (kext.shape[0], LANES), 1)
    lo = key_lane < HEAD_DIM_A
    in_a = key_lane % HALF_TILE < QUARTER_TILE
    r_idx = lax.broadcasted_iota(jnp.int32, (rows_u, n_keys), 0)
    k_idx = lax.broadcasted_iota(jnp.int32, (rows_u, n_keys), 1)
    first_pair = (lax.broadcasted_iota(jnp.int32, (rows_u, 1), 0) // CHUNK) % 2 == 0
    ci = r_idx // (2 * CHUNK)
    band = (k_idx >= ci * CHUNK) & (k_idx < (ci + 3) * CHUNK)

    kv_ops = []
    for kv in range(N_KV_A):
        col, odd = kv // 2, kv % 2
        kc = kext[:, col * LANES:(col + 1) * LANES]
        vc = vext[:, col * LANES:(col + 1) * LANES]
        vs = pltpu.roll(vc, HEAD_DIM_A, axis=1)
        if odd:
            kc, ks, vc, vs = pltpu.roll(kc, LANES - QUARTER_TILE, axis=1), kc, vs, vc
        else:
            ks = pltpu.roll(kc, QUARTER_TILE, axis=1)
        kv_ops.append(((jnp.where(in_a, kc, 0.0).astype(BF16), jnp.where(lo, vc, 0.0).astype(BF16)),
                       (jnp.where(in_a, 0.0, ks).astype(BF16), jnp.where(lo, 0.0, vs).astype(BF16))))

    units = [(g, kv, half) for g in range(n_ch // gs) for kv in range(N_KV_A) for half in range(2)]
    st = {}

    def scores(g, kv, half):
        base = kv * GROUP_A * HEAD_DIM_A
        start = g * gs * CHUNK
        k_m = kv_ops[kv][half][0][start:start + n_keys]
        qst = jnp.concatenate(
            [q_ref[(g * gs + c) * CHUNK:(g * gs + c + 1) * CHUNK, base + pair * LANES:base + (pair + 1) * LANES]
             for c in range(gs) for pair in range(2)], axis=0)
        st["s", g, kv, half] = _dot_nt(qst, k_m)

    def softmax(g, kv, half):
        start = g * gs * CHUNK
        s = st.pop(("s", g, kv, half))
        ok = None
        if gs > 1:
            ok = band
        if not prev_valid and start < WINDOW:
            valid = k_idx + start >= WINDOW
            ok = valid if ok is None else ok & valid
        if ok is not None:
            s = jnp.where(ok, s, NEG_INF)
        sink = jnp.where(first_pair, sinks_ref[kv * GROUP_A + half], sinks_ref[kv * GROUP_A + 2 + half])
        m = jnp.maximum(jnp.max(s, axis=-1, keepdims=True), sink)
        p = jnp.exp2(s - m)
        den = jnp.sum(p, axis=-1, keepdims=True) + jnp.exp2(sink - m)
        st[g, kv, half] = (p.astype(BF16), 1.0 / den)

    def values(g, kv, half):
        base = kv * GROUP_A * HEAD_DIM_A
        start = g * gs * CHUNK
        p, rden = st.pop((g, kv, half))
        v_m = kv_ops[kv][half][1][start:start + n_keys]
        o = jnp.dot(p, v_m, preferred_element_type=F32) * rden
        if half == 0:
            st[g, kv] = o
            return
        acc = st.pop((g, kv)) + o
        for c in range(gs):
            rows = slice((g * gs + c) * CHUNK, (g * gs + c + 1) * CHUNK)
            for pair in range(2):
                sl = slice(base + pair * LANES, base + (pair + 1) * LANES)
                blk = acc[(2 * c + pair) * CHUNK:(2 * c + pair + 1) * CHUNK]
                o_ref[rows, sl] = (blk * gate_ref[rows, sl].astype(F32)).astype(o_ref.dtype)

    for thunk in _lagged([[functools.partial(fn, *u) for u in units] for fn in (scores, softmax, values)],
                         ATTN_LAG):
        thunk()


def _attn_kernel(*refs, n_ch, has_hist):
    if has_hist:
        _attn_body(*refs, n_ch=n_ch, prev_valid=True)
        return
    t = pl.program_id(1)

    @pl.when(t == 0)
    def _():
        _attn_body(*refs, n_ch=n_ch, prev_valid=False)

    @pl.when(t > 0)
    def _():
        _attn_body(*refs, n_ch=n_ch, prev_valid=True)


def _attention(sinks2, q, k, v, gate, k_hist, v_hist, bsz, n_c, n_ch):
    n = q.shape[0]
    n_t = n_c // n_ch
    tok = n_ch * CHUNK
    cur = lambda w: pl.BlockSpec((tok, w), lambda b, t: (b * n_t + t, 0))
    if k_hist is None:
        per_seq = n_c * CHUNK // WINDOW
        prev = pl.BlockSpec((WINDOW, KV_WIDTH_A),
                            lambda b, t: (b * per_seq + jnp.maximum(t * (tok // WINDOW) - 1, 0), 0))
        kp, vp = k, v
    else:
        assert n_t == 1
        prev = pl.BlockSpec((WINDOW, KV_WIDTH_A), lambda b, t: (b, 0))
        kp, vp = k_hist, v_hist
    return pl.pallas_call(
        functools.partial(_attn_kernel, n_ch=n_ch, has_hist=k_hist is not None),
        grid=(bsz, n_t),
        in_specs=[pl.BlockSpec(memory_space=pltpu.SMEM),
                  cur(WIDTH_A), prev, cur(KV_WIDTH_A), prev, cur(KV_WIDTH_A), cur(WIDTH_A)],
        out_specs=cur(WIDTH_A),
        out_shape=jax.ShapeDtypeStruct((n, WIDTH_A), BF16),
        compiler_params=pltpu.CompilerParams(dimension_semantics=("parallel", "parallel"),
                                             vmem_limit_bytes=VMEM_LIMIT),
        name="band_attention",
    )(sinks2, q, kp, k, vp, v, gate)


def _lane_col(x, lane_idx, j):
    return jnp.sum(jnp.where(lane_idx == j, x, 0.0), axis=-1, keepdims=True)


def _delta_kernel(*refs, has_state, n_ch):
    (cs_ref, ab_ref, gate_ref, alog_ref, dtb_ref, nw_ref,
     oa_ref, sga_ref, sgb_ref, x_ref, woa_ref, wob_ref, wout_ref, lng_ref, lnb_ref) = refs[:15]
    if has_state:
        s0_ref, y_ref, s_ref, ob_ref = refs[15:]
    else:
        y_ref, s_ref, ob_ref = refs[15:]

    @pl.when(pl.program_id(1) == 0)
    def _():
        if has_state:
            s_ref[...] = s0_ref[...]
        else:
            s_ref[...] = jnp.zeros_like(s_ref)

    ri = lax.broadcasted_iota(jnp.int32, (CHUNK, LANES), 0)
    li = lax.broadcasted_iota(jnp.int32, (CHUNK, LANES), 1)
    ci = li % CHUNK
    lo = li < CHUNK
    lo_row = lax.broadcasted_iota(jnp.int32, (1, LANES), 1) < CHUNK
    causal = ri >= ci
    strict = ri > ci
    eye2 = jnp.where(ri == ci, 1.0, 0.0).astype(F32)
    si = lax.broadcasted_iota(jnp.int32, (LANES, LANES), 0)
    sl = lax.broadcasted_iota(jnp.int32, (LANES, LANES), 1)
    cum2 = jnp.where((si < CHUNK) & (si <= sl % CHUNK), 1.0, 0.0).astype(F32)
    last_lane = lax.broadcasted_iota(jnp.int32, (N_HEADS_B, LANES), 1) == CHUNK - 1
    heads = range(N_HEADS_B)
    pairs = range(N_HEADS_B // 2)
    zero_rhs = jnp.zeros((CHUNK, 2 * HEAD_DV_B), BF16)
    zero_v = jnp.zeros((CHUNK, HEAD_DV_B), BF16)
    zero_k = jnp.zeros((CHUNK, HEAD_DK_B), BF16)
    ops = {(c, p): {} for c in range(n_ch) for p in pairs}
    gates = [None] * n_ch

    def blockdiag(x):
        return jnp.concatenate([jnp.where(lo, x, 0.0), jnp.where(lo, 0.0, x)], axis=0).astype(BF16)

    def gate_stage(c):
        ab = ab_ref[c * CHUNK:(c + 1) * CHUNK, :]
        ab_t = jnp.concatenate([ab, ab], axis=0).T
        g_t = -jnp.exp(alog_ref[...]) * _softplus(ab_t[0:N_HEADS_B] + dtb_ref[...])
        gc_t = jnp.dot(g_t, cum2, preferred_element_type=F32, precision=lax.Precision.HIGHEST)
        gl = jnp.sum(jnp.where(last_lane, gc_t, 0.0), axis=-1, keepdims=True)
        pad = jnp.zeros((LANES - 3 * N_HEADS_B, LANES), F32)
        packed = jnp.concatenate([gc_t, jnp.exp(gc_t), -_sigmoid(ab_t[N_HEADS_B:2 * N_HEADS_B]), pad],
                                 axis=0)
        gates[c] = dict(gc_t=gc_t, cols=packed.T[:CHUNK], e_rev_t=jnp.exp(gl - gc_t),
                        e_gl=jnp.broadcast_to(jnp.exp(gl), (N_HEADS_B, LANES)))

    def prep(c, p):
        gt, op = gates[c], ops[c, p]
        rows = slice(c * CHUNK, (c + 1) * CHUNK)
        slab = lambda j: cs_ref[rows, j * LANES:(j + 1) * LANES].astype(F32)
        per = []
        for h in (2 * p, 2 * p + 1):
            col = lambda i: _lane_col(gt["cols"], li, N_HEADS_B * i + h)
            q, k, v = slab(h), slab(N_HEADS_B + h), slab(2 * N_HEADS_B + h)
            q = q * (lax.rsqrt(jnp.sum(q * q, axis=-1, keepdims=True) + L2_EPS) * (HEAD_DK_B ** -0.5))
            k = k * lax.rsqrt(jnp.sum(k * k, axis=-1, keepdims=True) + L2_EPS)
            gc_col, e_gc, nb = col(0), col(1), col(2)
            kbn = k * nb
            per.append(dict(
                lhs=jnp.concatenate([kbn.astype(BF16), q.astype(BF16)], axis=0),
                k=k.astype(BF16),
                rhs_uw=jnp.concatenate([(v * -nb).astype(BF16), (kbn * e_gc).astype(BF16)], axis=1),
                q_dec=(q * e_gc).astype(BF16), k_f32=k, gc_col=gc_col,
                e_gl=gt["e_gl"][h:h + 1, :]))
        a, b = per
        pair_row = lambda t: jnp.where(lo_row, t[2 * p:2 * p + 1, :], t[2 * p + 1:2 * p + 2, :])
        gc_row = pair_row(gt["gc_t"])
        diff = jnp.where(lo, a["gc_col"], b["gc_col"]) - gc_row
        op.update(
            lhs=jnp.concatenate([a["lhs"], b["lhs"]], axis=1),
            k_bd=jnp.concatenate([jnp.concatenate([a["k"], zero_k], axis=1),
                                  jnp.concatenate([zero_k, b["k"]], axis=1)], axis=0),
            rhs_uw=(jnp.concatenate([a["rhs_uw"], zero_rhs], axis=0),
                    jnp.concatenate([zero_rhs, b["rhs_uw"]], axis=0)),
            q_dec=(a["q_dec"], b["q_dec"]),
            k_dec_t=(jnp.concatenate([a["k_f32"], b["k_f32"]], axis=0).T * pair_row(gt["e_rev_t"])).astype(BF16),
            decay=jnp.where(causal, jnp.exp(jnp.where(causal, diff, 0.0)), 0.0),
            e_gl=(a["e_gl"], b["e_gl"]))

    def scores(c, p):
        op = ops[c, p]
        aq = _dot_nt(op["lhs"], op["k_bd"])
        op["p"] = jnp.where(strict, aq[:CHUNK] * op["decay"], 0.0)
        op["qk"] = (aq[CHUNK:] * op["decay"]).astype(BF16)

    def inv_first(c, p):
        op = ops[c, p]
        op["t"] = eye2 + op["p"]
        op["p"] = jnp.dot(op["p"].astype(BF16), blockdiag(op["p"]), preferred_element_type=F32)

    def inv_mid(c, p):
        op = ops[c, p]
        r = jnp.dot(jnp.concatenate([op["p"].astype(BF16), op["t"].astype(BF16)], axis=0), blockdiag(op["p"]),
                    preferred_element_type=F32)
        op["p"] = r[:CHUNK]
        op["t"] = op["t"] + r[CHUNK:]

    def inv_last(c, p):
        op = ops[c, p]
        op["t"] = op["t"] + jnp.dot(op["t"].astype(BF16), blockdiag(op["p"]), preferred_element_type=F32)

    def apply_inv(c, p):
        op = ops[c, p]
        tb = op["t"].astype(BF16)
        op["u"], op["wq"] = [], []
        for j in range(2):
            uw = jnp.dot(tb, op["rhs_uw"][j], preferred_element_type=F32)
            op["u"].append(uw[:, :HEAD_DV_B])
            op["wq"].append(jnp.concatenate([uw[:, HEAD_DV_B:].astype(BF16), op["q_dec"][j]], axis=0))

    state = [None] * N_HEADS_B

    def load_state(h):
        state[h] = s_ref[0, h]

    def rec_first(c, h):
        op, j = ops[c, h // 2], h % 2
        ws = _dot(op["wq"][j], state[h])
        op["v_new", j] = (op["u"][j] + ws[:CHUNK]).astype(BF16)
        op["qs", j] = ws[CHUNK:]

    def rec_second(c, p):
        op = ops[c, p]
        rows = slice(c * CHUNK, (c + 1) * CHUNK)
        v_bd = jnp.concatenate([jnp.concatenate([op["v_new", 0], zero_v], axis=1),
                                jnp.concatenate([zero_v, op["v_new", 1]], axis=1)], axis=0)
        r = jnp.dot(jnp.concatenate([op["qk"], op["k_dec_t"]], axis=0), v_bd,
                    preferred_element_type=F32)
        for j in range(2):
            h = 2 * p + j
            hs = slice(h * HEAD_DV_B, (h + 1) * HEAD_DV_B)
            rj = r[:, j * HEAD_DV_B:(j + 1) * HEAD_DV_B]
            o = op["qs", j] + rj[:CHUNK]
            state[h] = state[h] * op["e_gl"][j] + rj[CHUNK:]
            o = o * lax.rsqrt(jnp.mean(o * o, axis=-1, keepdims=True) + RMS_EPS) * nw_ref[...]
            ob_ref[rows, hs] = (o * gate_ref[rows, hs].astype(F32)).astype(ob_ref.dtype)

    col_blocks = [slice(j * OUT_COLS, (j + 1) * OUT_COLS) for j in range(D_MODEL // OUT_COLS)]
    outs = {}

    def mix(chunks, cb):
        rows = slice(chunks[0] * CHUNK, (chunks[-1] + 1) * CHUNK)
        y_a = jnp.dot(oa_ref[rows, :], woa_ref[:, cb], preferred_element_type=F32)
        y_b = jnp.dot(ob_ref[rows, :], wob_ref[:, cb], preferred_element_type=F32)
        hmix = sga_ref[rows, cb].astype(F32) * y_a + sgb_ref[rows, cb].astype(F32) * y_b
        outs.setdefault(chunks[0], []).append(hmix.astype(BF16))

    def join(chunks):
        outs[chunks[0]] = jnp.concatenate(outs[chunks[0]], axis=1)

    def project(chunks, cb):
        outs.setdefault(("sub", chunks[0]), []).append(
            jnp.dot(outs[chunks[0]], wout_ref[:, cb], preferred_element_type=F32))

    def norm(chunks):
        rows = slice(chunks[0] * CHUNK, (chunks[-1] + 1) * CHUNK)
        del outs[chunks[0]]
        r = ALPHA * x_ref[rows, :] + jnp.concatenate(outs.pop(("sub", chunks[0])), axis=1)
        mu = jnp.mean(r, axis=-1, keepdims=True)
        d = r - mu
        var = jnp.mean(d * d, axis=-1, keepdims=True)
        y_ref[rows, :] = d * lax.rsqrt(var + LN_EPS) * lng_ref[...] + lnb_ref[...]

    def output_stream(chunks):
        span = OUT_GROUPS * len(chunks)
        if (chunks[-1] + 1) % span and chunks[-1] + 1 != n_ch:
            return []
        first = chunks[-1] + 1 - span if (chunks[-1] + 1) % span == 0 else (chunks[-1] // span) * span
        chunks = list(range(first, chunks[-1] + 1))
        return ([functools.partial(mix, chunks, cb) for cb in col_blocks] + [functools.partial(join, chunks)]
                + [functools.partial(project, chunks, cb) for cb in col_blocks] + [functools.partial(norm, chunks)])

    def over(fn, chunks, idx):
        return [functools.partial(fn, c, i) for c in chunks for i in idx]

    def prep_stream(chunks):
        return [functools.partial(gate_stage, c) for c in chunks] + over(prep, chunks, pairs)

    def level_stream(levels):
        return lambda chunks: [t for fn in levels for t in over(fn, chunks, pairs)]

    def recurrence_stream(chunks):
        return [t for c in chunks for t in over(rec_first, [c], heads) + over(rec_second, [c], pairs)]

    stages = [prep_stream,
              level_stream([scores, inv_first, inv_mid]),
              level_stream([inv_mid, inv_mid, inv_mid]),
              level_stream([inv_last, apply_inv]),
              recurrence_stream,
              output_stream]
    group = DELTA_GROUP if n_ch % DELTA_GROUP == 0 else 1
    groups = [list(range(g * group, (g + 1) * group)) for g in range(n_ch // group)]
    for h in heads:
        load_state(h)
    for step in range(len(groups) + len(stages) - 1):
        streams = [stage(groups[step - k]) for k, stage in reversed(list(enumerate(stages)))
                   if 0 <= step - k < len(groups)]
        for thunk in _interleave(streams):
            thunk()
    for h in heads:
        s_ref[0, h] = state[h]


def _delta_out(cs, ab, gate, alog_col, dtb_col, nw_row, oa, sga, sgb, x2d, woa, wob, wout, lng, lnb, s0,
               bsz, n_c, n_ch):
    n = cs.shape[0]
    n_t = n_c // n_ch
    cur = lambda w: pl.BlockSpec((n_ch * CHUNK, w), lambda b, t: (b * n_t + t, 0))
    const = lambda shape: pl.BlockSpec(shape, lambda b, t: (0,) * len(shape))
    weight = pl.BlockSpec((D_MODEL, D_MODEL), lambda b, t: (0, 0), pipeline_mode=pl.Buffered(1))
    state = pl.BlockSpec((1, N_HEADS_B, HEAD_DK_B, HEAD_DV_B), lambda b, t: (b, 0, 0, 0))
    in_specs = [cur(CONV_CH), cur(AB_PAD), cur(WIDTH_B),
                const((N_HEADS_B, 1)), const((N_HEADS_B, 1)), const((1, HEAD_DV_B)),
                cur(WIDTH_A), cur(D_MODEL), cur(D_MODEL), cur(D_MODEL), weight, weight, weight,
                const((1, D_MODEL)), const((1, D_MODEL))]
    args = [cs, ab, gate, alog_col, dtb_col, nw_row, oa, sga, sgb, x2d, woa, wob, wout, lng, lnb]
    has_state = s0 is not None
    if has_state:
        in_specs.append(state)
        args.append(s0)
    return pl.pallas_call(
        functools.partial(_delta_kernel, has_state=has_state, n_ch=n_ch),
        grid=(bsz, n_t),
        in_specs=in_specs,
        out_specs=[cur(D_MODEL), state],
        out_shape=[jax.ShapeDtypeStruct((n, D_MODEL), F32),
                   jax.ShapeDtypeStruct((bsz, N_HEADS_B, HEAD_DK_B, HEAD_DV_B), F32)],
        scratch_shapes=[pltpu.VMEM((n_ch * CHUNK, WIDTH_B), BF16)],
        compiler_params=pltpu.CompilerParams(dimension_semantics=("parallel", "arbitrary"),
                                             vmem_limit_bytes=VMEM_LIMIT),
        name="gated_delta_out",
    )(*args)


def _rope_tables(pos):
    half = ROT_DIM // 2
    inv_freq = jnp.power(ROPE_THETA, -jnp.arange(half, dtype=F32) / half)
    ang = pos[:, None] * inv_freq[None, :]
    cos, sin = jnp.cos(ang), jnp.sin(ang)
    n = pos.shape[0]
    ones = jnp.ones((n, QUARTER_TILE - half), F32)
    zeros = jnp.zeros((n, QUARTER_TILE - half), F32)
    ct = jnp.concatenate([cos, ones] * 4, axis=1)
    sp = jnp.concatenate([-sin, zeros] * 2 + [sin, zeros] * 2, axis=1)
    return ct, sp


def _layer(x, pos_offset, k_hist, v_hist, conv_hist, s0, weights):
    (w_r, sinks2, conv_w, alog_col, dtb_col, nw_row, woa, wob, wout, lng, lnb) = weights
    bsz, l_len, _ = x.shape
    n = bsz * l_len
    n_c = l_len // CHUNK
    n_ch = min(ATTN_TILE, l_len) // CHUNK
    x2d = x.reshape(n, D_MODEL)
    pos = jnp.arange(l_len, dtype=F32) + pos_offset
    tables = _rope_tables(pos)
    if k_hist is None:
        tm = min(PROJ_TILE, l_len)
        kh = vh = hist8 = None
    else:
        tm = min(PROJ_TILE, n)
        assert tm % l_len == 0 and l_len >= SUBLANES
        tables = [jnp.tile(t, (tm // l_len, 1)) for t in tables]
        kh = _permute_tiles(k_hist.reshape(bsz * WINDOW, KV_WIDTH_A), ROPE_TILE_RUNS)
        vh = v_hist.reshape(bsz * WINDOW, KV_WIDTH_A)
        hist8 = jnp.pad(conv_hist, ((0, 0), (SUBLANES - (CONV_W - 1), 0), (0, 0)))
    q, k, v, gza, cs, gzb, sga, sgb, ab, tail = _projection(x2d, hist8, w_r, *tables, conv_w, tm, l_len)
    oa = _attention(sinks2, q, k, v, gza, kh, vh, bsz, n_c, n_ch)
    y, s_new = _delta_out(cs, ab, gzb, alog_col, dtb_col, nw_row, oa, sga, sgb, x2d, woa, wob, wout, lng, lnb, s0,
                          bsz, n_c, min(DELTA_TILE, l_len) // CHUNK)

    def split_heads(t):
        return jnp.stack([t[..., j * HEAD_DIM_A:(j + 1) * HEAD_DIM_A] for j in range(N_KV_A)], axis=2)

    k3 = k.reshape(bsz, l_len, KV_WIDTH_A)
    v3 = v.reshape(bsz, l_len, KV_WIDTH_A)
    if k_hist is None:
        new_k = split_heads(_permute_tiles(k3[:, l_len - WINDOW:], ROPE_TILE_RUNS_INV))
        new_v = split_heads(v3[:, l_len - WINDOW:])
    else:
        keep = k_hist.shape[1]
        new_k = jnp.concatenate([k_hist[:, l_len:], split_heads(_permute_tiles(k3, ROPE_TILE_RUNS_INV))], axis=1)
        new_v = jnp.concatenate([v_hist[:, l_len:], split_heads(v3)], axis=1)
        assert new_k.shape[1] == keep == WINDOW
    new_conv = tail.reshape(bsz, -1, SUBLANES, CONV_CH)[:, -1, SUBLANES - (CONV_W - 1):]
    return y.reshape(bsz, l_len, D_MODEL), new_k, new_v, new_conv, s_new


def kernel(x_prompt, x_sample, cache_attn_k, cache_attn_v, state_conv, state_delta, w_in, attn_sinks, conv_w,
           a_log, dt_bias, delta_norm_w, w_o_attn, w_o_delta, w_out, ln_g, ln_b):
    splits = np.cumsum((WIDTH_A, KV_WIDTH_A, KV_WIDTH_A, WIDTH_A, CONV_CH, N_HEADS_B, N_HEADS_B, WIDTH_B,
                        D_MODEL, D_MODEL))
    ab_lo, ab_hi = int(splits[4]), int(splits[6])
    w_r = jnp.concatenate([_permute_tiles(w_in[:, :OFF_V], ROPE_TILE_RUNS), w_in[:, OFF_V:ab_lo], w_in[:, ab_hi:],
                           w_in[:, ab_lo:ab_hi],
                           jnp.zeros((D_MODEL, AB_PAD - (ab_hi - ab_lo)), w_in.dtype)], axis=1).astype(BF16)
    head_col = lambda t: t.astype(F32).reshape(N_HEADS_B, 1)
    weights = (w_r, attn_sinks.astype(F32) * LOG2E, conv_w.astype(F32), head_col(a_log), head_col(dt_bias),
               delta_norm_w.astype(F32).reshape(1, HEAD_DV_B), w_o_attn.astype(BF16), w_o_delta.astype(BF16),
               w_out.astype(BF16), ln_g.astype(F32).reshape(1, D_MODEL), ln_b.astype(F32).reshape(1, D_MODEL))
    yp, kp, vp, cp, sp = _layer(x_prompt, 0.0, None, None, None, None, weights)
    ys, ks, vs, cs, ss = _layer(x_sample, float(PAST_LEN), cache_attn_k, cache_attn_v, state_conv, state_delta,
                                weights)
    return (yp, ys, kp, vp, cp, sp, ks, vs, cs, ss)
```

```python
import functools
import math

import numpy as np
import jax
import jax.numpy as jnp
from jax import lax
from jax.experimental import pallas as pl
from jax.experimental.pallas import tpu as pltpu

D_MODEL = 1024
CHUNK = 64
N_HEADS_A = 16
N_KV_A = 4
HEAD_DIM_A = 64
GROUP_A = N_HEADS_A // N_KV_A
WINDOW = 128
ROT_DIM = HEAD_DIM_A // 4
ROPE_THETA = 500000.0
WIDTH_A = N_HEADS_A * HEAD_DIM_A
KV_WIDTH_A = N_KV_A * HEAD_DIM_A
N_HEADS_B = 8
HEAD_DK_B = 128
HEAD_DV_B = 128
CONV_W = 4
QK_WIDTH_B = N_HEADS_B * HEAD_DK_B
WIDTH_B = N_HEADS_B * HEAD_DV_B
CONV_CH = 2 * QK_WIDTH_B + WIDTH_B
DEPTH = 1
PAST_LEN = 1024
ALPHA = (2.0 * DEPTH) ** 0.25
LN_EPS = 1e-5
RMS_EPS = 1e-6
L2_EPS = 1e-6
NEG_INF = -1e30
LOG2E = math.log2(math.e)

LANES = 128
SUBLANES = 8
AB_PAD = LANES
OFF_Q = 0
OFF_K = OFF_Q + WIDTH_A
OFF_V = OFF_K + KV_WIDTH_A
OFF_ZA = OFF_V + KV_WIDTH_A
OFF_QKVB = OFF_ZA + WIDTH_A
OFF_ZB = OFF_QKVB + CONV_CH
OFF_GA = OFF_ZB + WIDTH_B
OFF_GB = OFF_GA + D_MODEL
OFF_AB = OFF_GB + D_MODEL
D_IN_PAD = OFF_AB + AB_PAD

VMEM_LIMIT = 56 * 1024 * 1024
PROJ_SEG = 512
EPI_WEIGHT_CONV, EPI_WEIGHT_ROPE, EPI_WEIGHT_ACT = 11, 16, 9
PROJ_TILE = 512
ATTN_TILE = 256
ATTN_LAG = 2
DELTA_TILE = 512
DELTA_GROUP = 2
OUT_COLS = 512
OUT_GROUPS = 4

F32 = jnp.float32
BF16 = jnp.bfloat16


def _sigmoid(x):
    return 0.5 + 0.5 * jnp.tanh(0.5 * x)


def _silu(x):
    h = 0.5 * x
    return h + h * jnp.tanh(h)


def _softplus(x):
    return jnp.maximum(x, 0.0) + jnp.log1p(jnp.exp(-jnp.abs(x)))


def _dot(a, b):
    return jnp.dot(a.astype(BF16), b.astype(BF16), preferred_element_type=F32)


def _dot_nt(a, b):
    return lax.dot_general(a.astype(BF16), b.astype(BF16), (((1,), (1,)), ((), ())),
                           preferred_element_type=F32)


def _interleave(streams):
    tagged = [((i + 0.5) / len(s), si, i, t) for si, s in enumerate(streams) for i, t in enumerate(s)]
    tagged.sort(key=lambda x: x[:3])
    return [t for *_, t in tagged]


def _spread(mm, epi):
    total = sum(w for e in epi for w, _ in e)
    order, queue, emitted = [], [], 0
    for i, m in enumerate(mm):
        order.append(m)
        if i > 0:
            queue.extend(epi[i - 1])
        while queue and emitted < total * i / len(mm):
            w, thunk = queue.pop(0)
            order.append(thunk)
            emitted += w
    queue.extend(epi[-1])
    order.extend(thunk for _, thunk in queue)
    return order


def _lagged(streams, lag):
    n = len(streams[0])
    order = []
    for i in range(n + lag * (len(streams) - 1)):
        for j, s in enumerate(streams):
            if 0 <= i - j * lag < n:
                order.append(s[i - j * lag])
    return order


def _rope_cols(t, ct, sa, sb):
    return t * ct + pltpu.roll(t, 8, axis=1) * sa + pltpu.roll(t, LANES - 8, axis=1) * sb


def _proj_kernel(*refs, tm, piece, tiles_per_seq, has_hist):
    if has_hist:
        (x_ref, hist_ref, w_ref, ct_ref, sa_ref, sb_ref, cw_ref,
         q_ref, k_ref, v_ref, gza_ref, cs_ref, gzb_ref, sga_ref, sgb_ref, ab_ref, tail_ref) = refs
        xb = x_ref[...].astype(BF16)
        xb_look = xb
    else:
        (x_ref, xprev_ref, w_ref, ct_ref, sa_ref, sb_ref, cw_ref,
         q_ref, k_ref, v_ref, gza_ref, cs_ref, gzb_ref, sga_ref, sgb_ref, ab_ref, tail_ref) = refs
        xb = x_ref[...].astype(BF16)
        xb_look = jnp.concatenate([xb, xprev_ref[...].astype(BF16)], axis=0)
        first_tile = pl.program_id(0) % tiles_per_seq == 0
    ct, sa, sb = ct_ref[...], sa_ref[...], sb_ref[...]
    res = {}

    def matmul(key, off, width, lhs):
        res[key] = jnp.dot(lhs, w_ref[:, off:off + width], preferred_element_type=F32)

    def slabs(width):
        return [slice(j * LANES, (j + 1) * LANES) for j in range(width // LANES)]

    def epi_q(key, off):
        r = res.pop(key)
        for sl in slabs(r.shape[1]):
            qj = _rope_cols(r[:, sl], ct, sa, sb) * (HEAD_DIM_A ** -0.5 * LOG2E)
            q_ref[:, off + sl.start:off + sl.stop] = qj.astype(q_ref.dtype)

    def epi_kv(key):
        r = res.pop(key)
        for sl in slabs(KV_WIDTH_A):
            k_ref[:, sl] = _rope_cols(r[:, sl], ct, sa, sb)
        v_ref[...] = r[:, KV_WIDTH_A:]

    def epi_act(key, fn, out_ref, off):
        r = res.pop(key)
        out_ref[:, off:off + r.shape[1]] = fn(r).astype(out_ref.dtype)

    def epi_conv(key, off, sl):
        r = res[key]
        cols = slice(off + sl.start, off + sl.stop)
        for i in range(tm // piece):
            rows = slice(i * piece, (i + 1) * piece)
            cur = r[rows, sl]
            if has_hist:
                look = hist_ref[i, :, cols]
            else:
                look = jnp.where(first_tile, 0.0, r[tm:tm + SUBLANES, sl])
            tail_ref[i * SUBLANES:(i + 1) * SUBLANES, cols] = cur[piece - SUBLANES:]
            xp = jnp.concatenate([look, cur], axis=0)
            acc = cur * cw_ref[CONV_W - 1:CONV_W, cols]
            for j in range(1, CONV_W):
                acc = acc + pltpu.roll(xp, j, axis=0)[SUBLANES:] * cw_ref[CONV_W - 1 - j:CONV_W - j, cols]
            cs_ref[rows, cols] = _silu(acc).astype(cs_ref.dtype)

    def epi_ab(key):
        ab_ref[...] = res.pop(key)

    mm, epi = [], []

    def add(key, off, width, epilogues, lhs=None):
        mm.append(functools.partial(matmul, key, off, width, xb if lhs is None else lhs))
        epi.append(epilogues)

    for j in range(CONV_CH // PROJ_SEG):
        add(("qkvb", j), OFF_QKVB + j * PROJ_SEG, PROJ_SEG,
            [(EPI_WEIGHT_CONV, functools.partial(epi_conv, ("qkvb", j), j * PROJ_SEG, sl))
             for sl in slabs(PROJ_SEG)], lhs=xb_look)
    for j in range(WIDTH_A // PROJ_SEG):
        add(("q", j), OFF_Q + j * PROJ_SEG, PROJ_SEG,
            [(EPI_WEIGHT_ROPE, functools.partial(epi_q, ("q", j), j * PROJ_SEG))])
    add("kv", OFF_K, 2 * KV_WIDTH_A, [(EPI_WEIGHT_ROPE // 2, functools.partial(epi_kv, "kv"))])
    for name, base, fn, out_ref in (("za", OFF_ZA, _silu, gza_ref), ("zb", OFF_ZB, _silu, gzb_ref),
                                    ("ga", OFF_GA, _sigmoid, sga_ref), ("gb", OFF_GB, _sigmoid, sgb_ref)):
        for j in range(D_MODEL // PROJ_SEG):
            add((name, j), base + j * PROJ_SEG, PROJ_SEG,
                [(EPI_WEIGHT_ACT, functools.partial(epi_act, (name, j), fn, out_ref, j * PROJ_SEG))])
    add("ab", OFF_AB, AB_PAD, [(1, functools.partial(epi_ab, "ab"))])
    for thunk in _spread(mm, epi):
        thunk()


def _projection(x2d, hist8, w_r, ct, sa, sb, conv_w, tm, seq_len):
    n = x2d.shape[0]
    n_tiles = n // tm
    n_t = ct.shape[0] // tm
    has_hist = hist8 is not None
    row = lambda w: pl.BlockSpec((tm, w), lambda i: (i, 0))
    tab = pl.BlockSpec((tm, LANES), lambda i: (i % n_t, 0))
    if has_hist:
        piece, tiles_per_seq = seq_len, 1
        look_spec = pl.BlockSpec((tm // piece, SUBLANES, CONV_CH), lambda i: (i, 0, 0))
        look = hist8
    else:
        piece, tiles_per_seq = tm, seq_len // tm
        blocks = tm // SUBLANES
        look_spec = pl.BlockSpec((SUBLANES, D_MODEL), lambda i: (jnp.maximum(i * blocks - 1, 0), 0))
        look = x2d
    tail_rows = tm // piece * SUBLANES
    widths = (WIDTH_A, KV_WIDTH_A, KV_WIDTH_A, WIDTH_A, CONV_CH, WIDTH_B, D_MODEL, D_MODEL, AB_PAD)
    dtypes = (BF16, F32, F32, BF16, BF16, BF16, BF16, BF16, F32)
    return pl.pallas_call(
        functools.partial(_proj_kernel, tm=tm, piece=piece, tiles_per_seq=tiles_per_seq, has_hist=has_hist),
        grid=(n_tiles,),
        in_specs=[row(D_MODEL), look_spec,
                  pl.BlockSpec((D_MODEL, D_IN_PAD), lambda i: (0, 0), pipeline_mode=pl.Buffered(1)),
                  tab, tab, tab, pl.BlockSpec((CONV_W, CONV_CH), lambda i: (0, 0))],
        out_specs=[row(w) for w in widths] + [pl.BlockSpec((tail_rows, CONV_CH), lambda i: (i, 0))],
        out_shape=[jax.ShapeDtypeStruct((n, w), d) for w, d in zip(widths, dtypes)]
        + [jax.ShapeDtypeStruct((n_tiles * tail_rows, CONV_CH), F32)],
        compiler_params=pltpu.CompilerParams(dimension_semantics=("parallel",),
                                             vmem_limit_bytes=VMEM_LIMIT),
        name="in_projection",
    )(x2d, look, w_r, ct, sa, sb, conv_w)


def _attn_body(sinks_ref, q_ref, kprev_ref, kcur_ref, vprev_ref, vcur_ref, gate_ref, o_ref,
               *, n_ch, prev_valid):
    gs = 2 if n_ch % 2 == 0 else 1
    n_keys = (gs + 2) * CHUNK
    rows_u = gs * 2 * CHUNK
    kext = jnp.concatenate([kprev_ref[...], kcur_ref[...]], axis=0)
    vext = jnp.concatenate([vprev_ref[...], vcur_ref[...]], axis=0)
    lo = lax.broadcasted_iota(jnp.int32, (kext.shape[0], LANES), 1) < HEAD_DIM_A
    r_idx = lax.broadcasted_iota(jnp.int32, (rows_u, n_keys), 0)
    k_idx = lax.broadcasted_iota(jnp.int32, (rows_u, n_keys), 1)
    first_pair = (lax.broadcasted_iota(jnp.int32, (rows_u, 1), 0) // CHUNK) % 2 == 0
    ci = r_idx // (2 * CHUNK)
    band = (k_idx >= ci * CHUNK) & (k_idx < (ci + 3) * CHUNK)

    kv_ops = []
    for kv in range(N_KV_A):
        col, odd = kv // 2, kv % 2
        kc = kext[:, col * LANES:(col + 1) * LANES]
        vc = vext[:, col * LANES:(col + 1) * LANES]
        ks = pltpu.roll(kc, HEAD_DIM_A, axis=1)
        vs = pltpu.roll(vc, HEAD_DIM_A, axis=1)
        if odd:
            kc, ks, vc, vs = ks, kc, vs, vc
        kv_ops.append(((jnp.where(lo, kc, 0.0).astype(BF16), jnp.where(lo, vc, 0.0).astype(BF16)),
                       (jnp.where(lo, 0.0, ks).astype(BF16), jnp.where(lo, 0.0, vs).astype(BF16))))

    units = [(g, kv, half) for g in range(n_ch // gs) for kv in range(N_KV_A) for half in range(2)]
    st = {}

    def scores(g, kv, half):
        base = kv * GROUP_A * HEAD_DIM_A
        start = g * gs * CHUNK
        k_m = kv_ops[kv][half][0][start:start + n_keys]
        qst = jnp.concatenate(
            [q_ref[(g * gs + c) * CHUNK:(g * gs + c + 1) * CHUNK, base + pair * LANES:base + (pair + 1) * LANES]
             for c in range(gs) for pair in range(2)], axis=0)
        st["s", g, kv, half] = _dot_nt(qst, k_m)

    def softmax(g, kv, half):
        start = g * gs * CHUNK
        s = st.pop(("s", g, kv, half))
        ok = None
        if gs > 1:
            ok = band
        if not prev_valid and start < WINDOW:
            valid = k_idx + start >= WINDOW
            ok = valid if ok is None else ok & valid
        if ok is not None:
            s = jnp.where(ok, s, NEG_INF)
        sink = jnp.where(first_pair, sinks_ref[kv * GROUP_A + half], sinks_ref[kv * GROUP_A + 2 + half])
        m = jnp.maximum(jnp.max(s, axis=-1, keepdims=True), sink)
        p = jnp.exp2(s - m)
        den = jnp.sum(p, axis=-1, keepdims=True) + jnp.exp2(sink - m)
        st[g, kv, half] = (p.astype(BF16), 1.0 / den)

    def values(g, kv, half):
        base = kv * GROUP_A * HEAD_DIM_A
        start = g * gs * CHUNK
        p, rden = st.pop((g, kv, half))
        v_m = kv_ops[kv][half][1][start:start + n_keys]
        o = jnp.dot(p, v_m, preferred_element_type=F32) * rden
        if half == 0:
            st[g, kv] = o
            return
        acc = st.pop((g, kv)) + o
        for c in range(gs):
            rows = slice((g * gs + c) * CHUNK, (g * gs + c + 1) * CHUNK)
            for pair in range(2):
                sl = slice(base + pair * LANES, base + (pair + 1) * LANES)
                blk = acc[(2 * c + pair) * CHUNK:(2 * c + pair + 1) * CHUNK]
                o_ref[rows, sl] = (blk * gate_ref[rows, sl].astype(F32)).astype(o_ref.dtype)

    for thunk in _lagged([[functools.partial(fn, *u) for u in units] for fn in (scores, softmax, values)],
                         ATTN_LAG):
        thunk()


def _attn_kernel(*refs, n_ch, has_hist):
    if has_hist:
        _attn_body(*refs, n_ch=n_ch, prev_valid=True)
        return
    t = pl.program_id(1)

    @pl.when(t == 0)
    def _():
        _attn_body(*refs, n_ch=n_ch, prev_valid=False)

    @pl.when(t > 0)
    def _():
        _attn_body(*refs, n_ch=n_ch, prev_valid=True)


def _attention(sinks2, q, k, v, gate, k_hist, v_hist, bsz, n_c, n_ch):
    n = q.shape[0]
    n_t = n_c // n_ch
    tok = n_ch * CHUNK
    cur = lambda w: pl.BlockSpec((tok, w), lambda b, t: (b * n_t + t, 0))
    if k_hist is None:
        per_seq = n_c * CHUNK // WINDOW
        prev = pl.BlockSpec((WINDOW, KV_WIDTH_A),
                            lambda b, t: (b * per_seq + jnp.maximum(t * (tok // WINDOW) - 1, 0), 0))
        kp, vp = k, v
    else:
        assert n_t == 1
        prev = pl.BlockSpec((WINDOW, KV_WIDTH_A), lambda b, t: (b, 0))
        kp, vp = k_hist, v_hist
    return pl.pallas_call(
        functools.partial(_attn_kernel, n_ch=n_ch, has_hist=k_hist is not None),
        grid=(bsz, n_t),
        in_specs=[pl.BlockSpec(memory_space=pltpu.SMEM),
                  cur(WIDTH_A), prev, cur(KV_WIDTH_A), prev, cur(KV_WIDTH_A), cur(WIDTH_A)],
        out_specs=cur(WIDTH_A),
        out_shape=jax.ShapeDtypeStruct((n, WIDTH_A), BF16),
        compiler_params=pltpu.CompilerParams(dimension_semantics=("parallel", "parallel"),
                                             vmem_limit_bytes=VMEM_LIMIT),
        name="band_attention",
    )(sinks2, q, kp, k, vp, v, gate)


def _lane_col(x, lane_idx, j):
    return jnp.sum(jnp.where(lane_idx == j, x, 0.0), axis=-1, keepdims=True)


def _delta_kernel(*refs, has_state, n_ch):
    (cs_ref, ab_ref, gate_ref, alog_ref, dtb_ref, nw_ref,
     oa_ref, sga_ref, sgb_ref, x_ref, woa_ref, wob_ref, wout_ref, lng_ref, lnb_ref) = refs[:15]
    if has_state:
        s0_ref, y_ref, s_ref, ob_ref = refs[15:]
    else:
        y_ref, s_ref, ob_ref = refs[15:]

    @pl.when(pl.program_id(1) == 0)
    def _():
        if has_state:
            s_ref[...] = s0_ref[...]
        else:
            s_ref[...] = jnp.zeros_like(s_ref)

    ri = lax.broadcasted_iota(jnp.int32, (CHUNK, LANES), 0)
    li = lax.broadcasted_iota(jnp.int32, (CHUNK, LANES), 1)
    ci = li % CHUNK
    lo = li < CHUNK
    lo_row = lax.broadcasted_iota(jnp.int32, (1, LANES), 1) < CHUNK
    causal = ri >= ci
    strict = ri > ci
    eye2 = jnp.where(ri == ci, 1.0, 0.0).astype(F32)
    si = lax.broadcasted_iota(jnp.int32, (LANES, LANES), 0)
    sl = lax.broadcasted_iota(jnp.int32, (LANES, LANES), 1)
    cum2 = jnp.where((si < CHUNK) & (si <= sl % CHUNK), 1.0, 0.0).astype(F32)
    last_lane = lax.broadcasted_iota(jnp.int32, (N_HEADS_B, LANES), 1) == CHUNK - 1
    heads = range(N_HEADS_B)
    pairs = range(N_HEADS_B // 2)
    zero_rhs = jnp.zeros((CHUNK, 2 * HEAD_DV_B), BF16)
    zero_v = jnp.zeros((CHUNK, HEAD_DV_B), BF16)
    zero_k = jnp.zeros((CHUNK, HEAD_DK_B), BF16)
    ops = {(c, p): {} for c in range(n_ch) for p in pairs}
    gates = [None] * n_ch

    def blockdiag(x):
        return jnp.concatenate([jnp.where(lo, x, 0.0), jnp.where(lo, 0.0, x)], axis=0).astype(BF16)

    def gate_stage(c):
        ab = ab_ref[c * CHUNK:(c + 1) * CHUNK, :]
        ab_t = jnp.concatenate([ab, ab], axis=0).T
        g_t = -jnp.exp(alog_ref[...]) * _softplus(ab_t[0:N_HEADS_B] + dtb_ref[...])
        gc_t = jnp.dot(g_t, cum2, preferred_element_type=F32, precision=lax.Precision.HIGHEST)
        gl = jnp.sum(jnp.where(last_lane, gc_t, 0.0), axis=-1, keepdims=True)
        pad = jnp.zeros((LANES - 3 * N_HEADS_B, LANES), F32)
        packed = jnp.concatenate([gc_t, jnp.exp(gc_t), -_sigmoid(ab_t[N_HEADS_B:2 * N_HEADS_B]), pad],
                                 axis=0)
        gates[c] = dict(gc_t=gc_t, cols=packed.T[:CHUNK], e_rev_t=jnp.exp(gl - gc_t),
                        e_gl=jnp.broadcast_to(jnp.exp(gl), (N_HEADS_B, LANES)))

    def prep(c, p):
        gt, op = gates[c], ops[c, p]
        rows = slice(c * CHUNK, (c + 1) * CHUNK)
        slab = lambda j: cs_ref[rows, j * LANES:(j + 1) * LANES].astype(F32)
        per = []
        for h in (2 * p, 2 * p + 1):
            col = lambda i: _lane_col(gt["cols"], li, N_HEADS_B * i + h)
            q, k, v = slab(h), slab(N_HEADS_B + h), slab(2 * N_HEADS_B + h)
            q = q * (lax.rsqrt(jnp.sum(q * q, axis=-1, keepdims=True) + L2_EPS) * (HEAD_DK_B ** -0.5))
            k = k * lax.rsqrt(jnp.sum(k * k, axis=-1, keepdims=True) + L2_EPS)
            gc_col, e_gc, nb = col(0), col(1), col(2)
            kbn = k * nb
            per.append(dict(
                lhs=jnp.concatenate([kbn.astype(BF16), q.astype(BF16)], axis=0),
                k=k.astype(BF16),
                rhs_uw=jnp.concatenate([(v * -nb).astype(BF16), (kbn * e_gc).astype(BF16)], axis=1),
                q_dec=(q * e_gc).astype(BF16), k_f32=k, gc_col=gc_col,
                e_gl=gt["e_gl"][h:h + 1, :]))
        a, b = per
        pair_row = lambda t: jnp.where(lo_row, t[2 * p:2 * p + 1, :], t[2 * p + 1:2 * p + 2, :])
        gc_row = pair_row(gt["gc_t"])
        diff = jnp.where(lo, a["gc_col"], b["gc_col"]) - gc_row
        op.update(
            lhs=jnp.concatenate([a["lhs"], b["lhs"]], axis=1),
            k_bd=jnp.concatenate([jnp.concatenate([a["k"], zero_k], axis=1),
                                  jnp.concatenate([zero_k, b["k"]], axis=1)], axis=0),
            rhs_uw=(jnp.concatenate([a["rhs_uw"], zero_rhs], axis=0),
                    jnp.concatenate([zero_rhs, b["rhs_uw"]], axis=0)),
            q_dec=(a["q_dec"], b["q_dec"]),
            k_dec_t=(jnp.concatenate([a["k_f32"], b["k_f32"]], axis=0).T * pair_row(gt["e_rev_t"])).astype(BF16),
            decay=jnp.where(causal, jnp.exp(jnp.where(causal, diff, 0.0)), 0.0),
            e_gl=(a["e_gl"], b["e_gl"]))

    def scores(c, p):
        op = ops[c, p]
        aq = _dot_nt(op["lhs"], op["k_bd"])
        op["p"] = jnp.where(strict, aq[:CHUNK] * op["decay"], 0.0)
        op["qk"] = (aq[CHUNK:] * op["decay"]).astype(BF16)

    def inv_first(c, p):
        op = ops[c, p]
        op["t"] = eye2 + op["p"]
        op["p"] = jnp.dot(op["p"].astype(BF16), blockdiag(op["p"]), preferred_element_type=F32)

    def inv_mid(c, p):
        op = ops[c, p]
        r = jnp.dot(jnp.concatenate([op["p"].astype(BF16), op["t"].astype(BF16)], axis=0), blockdiag(op["p"]),
                    preferred_element_type=F32)
        op["p"] = r[:CHUNK]
        op["t"] = op["t"] + r[CHUNK:]

    def inv_last(c, p):
        op = ops[c, p]
        op["t"] = op["t"] + jnp.dot(op["t"].astype(BF16), blockdiag(op["p"]), preferred_element_type=F32)

    def apply_inv(c, p):
        op = ops[c, p]
        tb = op["t"].astype(BF16)
        op["u"], op["wq"] = [], []
        for j in range(2):
            uw = jnp.dot(tb, op["rhs_uw"][j], preferred_element_type=F32)
            op["u"].append(uw[:, :HEAD_DV_B])
            op["wq"].append(jnp.concatenate([uw[:, HEAD_DV_B:].astype(BF16), op["q_dec"][j]], axis=0))

    state = [None] * N_HEADS_B

    def load_state(h):
        state[h] = s_ref[0, h]

    def rec_first(c, h):
        op, j = ops[c, h // 2], h % 2
        ws = _dot(op["wq"][j], state[h])
        op["v_new", j] = (op["u"][j] + ws[:CHUNK]).astype(BF16)
        op["qs", j] = ws[CHUNK:]

    def rec_second(c, p):
        op = ops[c, p]
        rows = slice(c * CHUNK, (c + 1) * CHUNK)
        v_bd = jnp.concatenate([jnp.concatenate([op["v_new", 0], zero_v], axis=1),
                                jnp.concatenate([zero_v, op["v_new", 1]], axis=1)], axis=0)
        r = jnp.dot(jnp.concatenate([op["qk"], op["k_dec_t"]], axis=0), v_bd,
                    preferred_element_type=F32)
        for j in range(2):
            h = 2 * p + j
            hs = slice(h * HEAD_DV_B, (h + 1) * HEAD_DV_B)
            rj = r[:, j * HEAD_DV_B:(j + 1) * HEAD_DV_B]
            o = op["qs", j] + rj[:CHUNK]
            state[h] = state[h] * op["e_gl"][j] + rj[CHUNK:]
            o = o * lax.rsqrt(jnp.mean(o * o, axis=-1, keepdims=True) + RMS_EPS) * nw_ref[...]
            ob_ref[rows, hs] = (o * gate_ref[rows, hs].astype(F32)).astype(ob_ref.dtype)

    col_blocks = [slice(j * OUT_COLS, (j + 1) * OUT_COLS) for j in range(D_MODEL // OUT_COLS)]
    outs = {}

    def mix(chunks, cb):
        rows = slice(chunks[0] * CHUNK, (chunks[-1] + 1) * CHUNK)
        y_a = jnp.dot(oa_ref[rows, :], woa_ref[:, cb], preferred_element_type=F32)
        y_b = jnp.dot(ob_ref[rows, :], wob_ref[:, cb], preferred_element_type=F32)
        hmix = sga_ref[rows, cb].astype(F32) * y_a + sgb_ref[rows, cb].astype(F32) * y_b
        outs.setdefault(chunks[0], []).append(hmix.astype(BF16))

    def join(chunks):
        outs[chunks[0]] = jnp.concatenate(outs[chunks[0]], axis=1)

    def project(chunks, cb):
        outs.setdefault(("sub", chunks[0]), []).append(
            jnp.dot(outs[chunks[0]], wout_ref[:, cb], preferred_element_type=F32))

    def norm(chunks):
        rows = slice(chunks[0] * CHUNK, (chunks[-1] + 1) * CHUNK)
        del outs[chunks[0]]
        r = ALPHA * x_ref[rows, :] + jnp.concatenate(outs.pop(("sub", chunks[0])), axis=1)
        mu = jnp.mean(r, axis=-1, keepdims=True)
        d = r - mu
        var = jnp.mean(d * d, axis=-1, keepdims=True)
        y_ref[rows, :] = d * lax.rsqrt(var + LN_EPS) * lng_ref[...] + lnb_ref[...]

    def output_stream(chunks):
        span = OUT_GROUPS * len(chunks)
        if (chunks[-1] + 1) % span and chunks[-1] + 1 != n_ch:
            return []
        first = chunks[-1] + 1 - span if (chunks[-1] + 1) % span == 0 else (chunks[-1] // span) * span
        chunks = list(range(first, chunks[-1] + 1))
        return ([functools.partial(mix, chunks, cb) for cb in col_blocks] + [functools.partial(join, chunks)]
                + [functools.partial(project, chunks, cb) for cb in col_blocks] + [functools.partial(norm, chunks)])

    def over(fn, chunks, idx):
        return [functools.partial(fn, c, i) for c in chunks for i in idx]

    def prep_stream(chunks):
        return [functools.partial(gate_stage, c) for c in chunks] + over(prep, chunks, pairs)

    def level_stream(levels):
        return lambda chunks: [t for fn in levels for t in over(fn, chunks, pairs)]

    def recurrence_stream(chunks):
        return [t for c in chunks for t in over(rec_first, [c], heads) + over(rec_second, [c], pairs)]

    stages = [prep_stream,
              level_stream([scores, inv_first, inv_mid]),
              level_stream([inv_mid, inv_mid, inv_mid]),
              level_stream([inv_last, apply_inv]),
              recurrence_stream,
              output_stream]
    group = DELTA_GROUP if n_ch % DELTA_GROUP == 0 else 1
    groups = [list(range(g * group, (g + 1) * group)) for g in range(n_ch // group)]
    for h in heads:
        load_state(h)
    for step in range(len(groups) + len(stages) - 1):
        streams = [stage(groups[step - k]) for k, stage in reversed(list(enumerate(stages)))
                   if 0 <= step - k < len(groups)]
        for thunk in _interleave(streams):
            thunk()
    for h in heads:
        s_ref[0, h] = state[h]


def _delta_out(cs, ab, gate, alog_col, dtb_col, nw_row, oa, sga, sgb, x2d, woa, wob, wout, lng, lnb, s0,
               bsz, n_c, n_ch):
    n = cs.shape[0]
    n_t = n_c // n_ch
    cur = lambda w: pl.BlockSpec((n_ch * CHUNK, w), lambda b, t: (b * n_t + t, 0))
    const = lambda shape: pl.BlockSpec(shape, lambda b, t: (0,) * len(shape))
    weight = pl.BlockSpec((D_MODEL, D_MODEL), lambda b, t: (0, 0), pipeline_mode=pl.Buffered(1))
    state = pl.BlockSpec((1, N_HEADS_B, HEAD_DK_B, HEAD_DV_B), lambda b, t: (b, 0, 0, 0))
    in_specs = [cur(CONV_CH), cur(AB_PAD), cur(WIDTH_B),
                const((N_HEADS_B, 1)), const((N_HEADS_B, 1)), const((1, HEAD_DV_B)),
                cur(WIDTH_A), cur(D_MODEL), cur(D_MODEL), cur(D_MODEL), weight, weight, weight,
                const((1, D_MODEL)), const((1, D_MODEL))]
    args = [cs, ab, gate, alog_col, dtb_col, nw_row, oa, sga, sgb, x2d, woa, wob, wout, lng, lnb]
    has_state = s0 is not None
    if has_state:
        in_specs.append(state)
        args.append(s0)
    return pl.pallas_call(
        functools.partial(_delta_kernel, has_state=has_state, n_ch=n_ch),
        grid=(bsz, n_t),
        in_specs=in_specs,
        out_specs=[cur(D_MODEL), state],
        out_shape=[jax.ShapeDtypeStruct((n, D_MODEL), F32),
                   jax.ShapeDtypeStruct((bsz, N_HEADS_B, HEAD_DK_B, HEAD_DV_B), F32)],
        scratch_shapes=[pltpu.VMEM((n_ch * CHUNK, WIDTH_B), BF16)],
        compiler_params=pltpu.CompilerParams(dimension_semantics=("parallel", "arbitrary"),
                                             vmem_limit_bytes=VMEM_LIMIT),
        name="gated_delta_out",
    )(*args)


def _rope_tables(pos):
    half = ROT_DIM // 2
    inv_freq = jnp.power(ROPE_THETA, -jnp.arange(half, dtype=F32) / half)
    ang = pos[:, None] * inv_freq[None, :]
    cos, sin = jnp.cos(ang), jnp.sin(ang)
    n = pos.shape[0]
    ones = jnp.ones((n, HEAD_DIM_A - ROT_DIM), F32)
    zeros = jnp.zeros((n, HEAD_DIM_A - ROT_DIM), F32)
    zh = jnp.zeros((n, half), F32)
    ct = jnp.concatenate([cos, cos, ones], axis=1)
    sa = jnp.concatenate([zh, sin, zeros], axis=1)
    sb = jnp.concatenate([-sin, zh, zeros], axis=1)
    tile2 = lambda t: jnp.concatenate([t, t], axis=1)
    return tile2(ct), tile2(sa), tile2(sb)


def _layer(x, pos_offset, k_hist, v_hist, conv_hist, s0, weights):
    (w_r, sinks2, conv_w, alog_col, dtb_col, nw_row, woa, wob, wout, lng, lnb) = weights
    bsz, l_len, _ = x.shape
    n = bsz * l_len
    n_c = l_len // CHUNK
    n_ch = min(ATTN_TILE, l_len) // CHUNK
    x2d = x.reshape(n, D_MODEL)
    pos = jnp.arange(l_len, dtype=F32) + pos_offset
    tables = _rope_tables(pos)
    if k_hist is None:
        tm = min(PROJ_TILE, l_len)
        kh = vh = hist8 = None
    else:
        tm = min(PROJ_TILE, n)
        assert tm % l_len == 0 and l_len >= SUBLANES
        tables = [jnp.tile(t, (tm // l_len, 1)) for t in tables]
        kh = k_hist.reshape(bsz * WINDOW, KV_WIDTH_A)
        vh = v_hist.reshape(bsz * WINDOW, KV_WIDTH_A)
        hist8 = jnp.pad(conv_hist, ((0, 0), (SUBLANES - (CONV_W - 1), 0), (0, 0)))
    q, k, v, gza, cs, gzb, sga, sgb, ab, tail = _projection(x2d, hist8, w_r, *tables, conv_w, tm, l_len)
    oa = _attention(sinks2, q, k, v, gza, kh, vh, bsz, n_c, n_ch)
    y, s_new = _delta_out(cs, ab, gzb, alog_col, dtb_col, nw_row, oa, sga, sgb, x2d, woa, wob, wout, lng, lnb, s0,
                          bsz, n_c, min(DELTA_TILE, l_len) // CHUNK)

    def split_heads(t):
        return jnp.stack([t[..., j * HEAD_DIM_A:(j + 1) * HEAD_DIM_A] for j in range(N_KV_A)], axis=2)

    k3 = k.reshape(bsz, l_len, KV_WIDTH_A)
    v3 = v.reshape(bsz, l_len, KV_WIDTH_A)
    if k_hist is None:
        new_k, new_v = split_heads(k3[:, l_len - WINDOW:]), split_heads(v3[:, l_len - WINDOW:])
    else:
        keep = k_hist.shape[1]
        new_k = jnp.concatenate([k_hist[:, l_len:], split_heads(k3)], axis=1)
        new_v = jnp.concatenate([v_hist[:, l_len:], split_heads(v3)], axis=1)
        assert new_k.shape[1] == keep == WINDOW
    new_conv = tail.reshape(bsz, -1, SUBLANES, CONV_CH)[:, -1, SUBLANES - (CONV_W - 1):]
    return y.reshape(bsz, l_len, D_MODEL), new_k, new_v, new_conv, s_new


def kernel(x_prompt, x_sample, cache_attn_k, cache_attn_v, state_conv, state_delta, w_in, attn_sinks, conv_w,
           a_log, dt_bias, delta_norm_w, w_o_attn, w_o_delta, w_out, ln_g, ln_b):
    splits = np.cumsum((WIDTH_A, KV_WIDTH_A, KV_WIDTH_A, WIDTH_A, CONV_CH, N_HEADS_B, N_HEADS_B, WIDTH_B,
                        D_MODEL, D_MODEL))
    ab_lo, ab_hi = int(splits[4]), int(splits[6])
    w_r = jnp.concatenate([w_in[:, :ab_lo], w_in[:, ab_hi:], w_in[:, ab_lo:ab_hi],
                           jnp.zeros((D_MODEL, AB_PAD - (ab_hi - ab_lo)), w_in.dtype)], axis=1).astype(BF16)
    head_col = lambda t: t.astype(F32).reshape(N_HEADS_B, 1)
    weights = (w_r, attn_sinks.astype(F32) * LOG2E, conv_w.astype(F32), head_col(a_log), head_col(dt_bias),
               delta_norm_w.astype(F32).reshape(1, HEAD_DV_B), w_o_attn.astype(BF16), w_o_delta.astype(BF16),
               w_out.astype(BF16), ln_g.astype(F32).reshape(1, D_MODEL), ln_b.astype(F32).reshape(1, D_MODEL))
    yp, kp, vp, cp, sp = _layer(x_prompt, 0.0, None, None, None, None, weights)
    ys, ks, vs, cs, ss = _layer(x_sample, float(PAST_LEN), cache_attn_k, cache_attn_v, state_conv, state_delta,
                                weights)
    return (yp, ys, kp, vp, cp, sp, ks, vs, cs, ss)
```

```python
import functools
import math

import numpy as np
import jax
import jax.numpy as jnp
from jax import lax
from jax.experimental import pallas as pl
from jax.experimental.pallas import tpu as pltpu

D_MODEL = 1024
CHUNK = 64
N_HEADS_A = 16
N_KV_A = 4
HEAD_DIM_A = 64
GROUP_A = N_HEADS_A // N_KV_A
WINDOW = 128
ROT_DIM = HEAD_DIM_A // 4
ROPE_THETA = 500000.0
WIDTH_A = N_HEADS_A * HEAD_DIM_A
KV_WIDTH_A = N_KV_A * HEAD_DIM_A
N_HEADS_B = 8
HEAD_DK_B = 128
HEAD_DV_B = 128
CONV_W = 4
QK_WIDTH_B = N_HEADS_B * HEAD_DK_B
WIDTH_B = N_HEADS_B * HEAD_DV_B
CONV_CH = 2 * QK_WIDTH_B + WIDTH_B
DEPTH = 1
PAST_LEN = 1024
ALPHA = (2.0 * DEPTH) ** 0.25
LN_EPS = 1e-5
RMS_EPS = 1e-6
L2_EPS = 1e-6
NEG_INF = -1e30
LOG2E = math.log2(math.e)

LANES = 128
SUBLANES = 8
AB_PAD = LANES
OFF_Q = 0
OFF_K = OFF_Q + WIDTH_A
OFF_V = OFF_K + KV_WIDTH_A
OFF_ZA = OFF_V + KV_WIDTH_A
OFF_QKVB = OFF_ZA + WIDTH_A
OFF_ZB = OFF_QKVB + CONV_CH
OFF_GA = OFF_ZB + WIDTH_B
OFF_GB = OFF_GA + D_MODEL
OFF_AB = OFF_GB + D_MODEL
D_IN_PAD = OFF_AB + AB_PAD

VMEM_LIMIT = 56 * 1024 * 1024
PROJ_SEG = 512
EPI_WEIGHT_CONV, EPI_WEIGHT_ROPE, EPI_WEIGHT_ACT = 11, 16, 9
PROJ_TILE = 512
ATTN_TILE = 512
ATTN_LAG = 2
DELTA_TILE = 512
DELTA_GROUP = 2
OUT_COLS = 512
OUT_GROUPS = 4

F32 = jnp.float32
BF16 = jnp.bfloat16


def _sigmoid(x):
    return 0.5 + 0.5 * jnp.tanh(0.5 * x)


def _silu(x):
    h = 0.5 * x
    return h + h * jnp.tanh(h)


def _softplus(x):
    return jnp.maximum(x, 0.0) + jnp.log1p(jnp.exp(-jnp.abs(x)))


def _dot(a, b):
    return jnp.dot(a.astype(BF16), b.astype(BF16), preferred_element_type=F32)


def _dot_nt(a, b):
    return lax.dot_general(a.astype(BF16), b.astype(BF16), (((1,), (1,)), ((), ())),
                           preferred_element_type=F32)


def _interleave(streams):
    tagged = [((i + 0.5) / len(s), si, i, t) for si, s in enumerate(streams) for i, t in enumerate(s)]
    tagged.sort(key=lambda x: x[:3])
    return [t for *_, t in tagged]


def _spread(mm, epi):
    total = sum(w for e in epi for w, _ in e)
    order, queue, emitted = [], [], 0
    for i, m in enumerate(mm):
        order.append(m)
        if i > 0:
            queue.extend(epi[i - 1])
        while queue and emitted < total * i / len(mm):
            w, thunk = queue.pop(0)
            order.append(thunk)
            emitted += w
    queue.extend(epi[-1])
    order.extend(thunk for _, thunk in queue)
    return order


def _lagged(streams, lag):
    n = len(streams[0])
    order = []
    for i in range(n + lag * (len(streams) - 1)):
        for j, s in enumerate(streams):
            if 0 <= i - j * lag < n:
                order.append(s[i - j * lag])
    return order


def _rope_cols(t, ct, sa, sb):
    return t * ct + pltpu.roll(t, 8, axis=1) * sa + pltpu.roll(t, LANES - 8, axis=1) * sb


def _proj_kernel(*refs, tm, piece, tiles_per_seq, has_hist):
    if has_hist:
        (x_ref, hist_ref, w_ref, ct_ref, sa_ref, sb_ref, cw_ref,
         q_ref, k_ref, v_ref, gza_ref, cs_ref, gzb_ref, sga_ref, sgb_ref, ab_ref, tail_ref) = refs
        xb = x_ref[...].astype(BF16)
        xb_look = xb
    else:
        (x_ref, xprev_ref, w_ref, ct_ref, sa_ref, sb_ref, cw_ref,
         q_ref, k_ref, v_ref, gza_ref, cs_ref, gzb_ref, sga_ref, sgb_ref, ab_ref, tail_ref) = refs
        xb = x_ref[...].astype(BF16)
        xb_look = jnp.concatenate([xb, xprev_ref[...].astype(BF16)], axis=0)
        first_tile = pl.program_id(0) % tiles_per_seq == 0
    ct, sa, sb = ct_ref[...], sa_ref[...], sb_ref[...]
    res = {}

    def matmul(key, off, width, lhs):
        res[key] = jnp.dot(lhs, w_ref[:, off:off + width], preferred_element_type=F32)

    def slabs(width):
        return [slice(j * LANES, (j + 1) * LANES) for j in range(width // LANES)]

    def epi_q(key, off):
        r = res.pop(key)
        for sl in slabs(r.shape[1]):
            qj = _rope_cols(r[:, sl], ct, sa, sb) * (HEAD_DIM_A ** -0.5 * LOG2E)
            q_ref[:, off + sl.start:off + sl.stop] = qj.astype(q_ref.dtype)

    def epi_kv(key):
        r = res.pop(key)
        for sl in slabs(KV_WIDTH_A):
            k_ref[:, sl] = _rope_cols(r[:, sl], ct, sa, sb)
        v_ref[...] = r[:, KV_WIDTH_A:]

    def epi_act(key, fn, out_ref, off):
        r = res.pop(key)
        out_ref[:, off:off + r.shape[1]] = fn(r).astype(out_ref.dtype)

    def epi_conv(key, off, sl):
        r = res[key]
        cols = slice(off + sl.start, off + sl.stop)
        for i in range(tm // piece):
            rows = slice(i * piece, (i + 1) * piece)
            cur = r[rows, sl]
            if has_hist:
                look = hist_ref[i, :, cols]
            else:
                look = jnp.where(first_tile, 0.0, r[tm:tm + SUBLANES, sl])
            tail_ref[i * SUBLANES:(i + 1) * SUBLANES, cols] = cur[piece - SUBLANES:]
            xp = jnp.concatenate([look, cur], axis=0)
            acc = cur * cw_ref[CONV_W - 1:CONV_W, cols]
            for j in range(1, CONV_W):
                acc = acc + pltpu.roll(xp, j, axis=0)[SUBLANES:] * cw_ref[CONV_W - 1 - j:CONV_W - j, cols]
            cs_ref[rows, cols] = _silu(acc).astype(cs_ref.dtype)

    def epi_ab(key):
        ab_ref[...] = res.pop(key)

    mm, epi = [], []

    def add(key, off, width, epilogues, lhs=None):
        mm.append(functools.partial(matmul, key, off, width, xb if lhs is None else lhs))
        epi.append(epilogues)

    for j in range(CONV_CH // PROJ_SEG):
        add(("qkvb", j), OFF_QKVB + j * PROJ_SEG, PROJ_SEG,
            [(EPI_WEIGHT_CONV, functools.partial(epi_conv, ("qkvb", j), j * PROJ_SEG, sl))
             for sl in slabs(PROJ_SEG)], lhs=xb_look)
    for j in range(WIDTH_A // PROJ_SEG):
        add(("q", j), OFF_Q + j * PROJ_SEG, PROJ_SEG,
            [(EPI_WEIGHT_ROPE, functools.partial(epi_q, ("q", j), j * PROJ_SEG))])
    add("kv", OFF_K, 2 * KV_WIDTH_A, [(EPI_WEIGHT_ROPE // 2, functools.partial(epi_kv, "kv"))])
    for name, base, fn, out_ref in (("za", OFF_ZA, _silu, gza_ref), ("zb", OFF_ZB, _silu, gzb_ref),
                                    ("ga", OFF_GA, _sigmoid, sga_ref), ("gb", OFF_GB, _sigmoid, sgb_ref)):
        for j in range(D_MODEL // PROJ_SEG):
            add((name, j), base + j * PROJ_SEG, PROJ_SEG,
                [(EPI_WEIGHT_ACT, functools.partial(epi_act, (name, j), fn, out_ref, j * PROJ_SEG))])
    add("ab", OFF_AB, AB_PAD, [(1, functools.partial(epi_ab, "ab"))])
    for thunk in _spread(mm, epi):
        thunk()


def _projection(x2d, hist8, w_r, ct, sa, sb, conv_w, tm, seq_len):
    n = x2d.shape[0]
    n_tiles = n // tm
    n_t = ct.shape[0] // tm
    has_hist = hist8 is not None
    row = lambda w: pl.BlockSpec((tm, w), lambda i: (i, 0))
    tab = pl.BlockSpec((tm, LANES), lambda i: (i % n_t, 0))
    if has_hist:
        piece, tiles_per_seq = seq_len, 1
        look_spec = pl.BlockSpec((tm // piece, SUBLANES, CONV_CH), lambda i: (i, 0, 0))
        look = hist8
    else:
        piece, tiles_per_seq = tm, seq_len // tm
        blocks = tm // SUBLANES
        look_spec = pl.BlockSpec((SUBLANES, D_MODEL), lambda i: (jnp.maximum(i * blocks - 1, 0), 0))
        look = x2d
    tail_rows = tm // piece * SUBLANES
    widths = (WIDTH_A, KV_WIDTH_A, KV_WIDTH_A, WIDTH_A, CONV_CH, WIDTH_B, D_MODEL, D_MODEL, AB_PAD)
    dtypes = (BF16, F32, F32, BF16, BF16, BF16, BF16, BF16, F32)
    return pl.pallas_call(
        functools.partial(_proj_kernel, tm=tm, piece=piece, tiles_per_seq=tiles_per_seq, has_hist=has_hist),
        grid=(n_tiles,),
        in_specs=[row(D_MODEL), look_spec,
                  pl.BlockSpec((D_MODEL, D_IN_PAD), lambda i: (0, 0), pipeline_mode=pl.Buffered(1)),
                  tab, tab, tab, pl.BlockSpec((CONV_W, CONV_CH), lambda i: (0, 0))],
        out_specs=[row(w) for w in widths] + [pl.BlockSpec((tail_rows, CONV_CH), lambda i: (i, 0))],
        out_shape=[jax.ShapeDtypeStruct((n, w), d) for w, d in zip(widths, dtypes)]
        + [jax.ShapeDtypeStruct((n_tiles * tail_rows, CONV_CH), F32)],
        compiler_params=pltpu.CompilerParams(dimension_semantics=("parallel",),
                                             vmem_limit_bytes=VMEM_LIMIT),
        name="in_projection",
    )(x2d, look, w_r, ct, sa, sb, conv_w)


def _attn_body(sinks_ref, q_ref, kprev_ref, kcur_ref, vprev_ref, vcur_ref, gate_ref, o_ref,
               *, n_ch, prev_valid):
    gs = 2 if n_ch % 2 == 0 else 1
    n_keys = (gs + 2) * CHUNK
    rows_u = gs * 2 * CHUNK
    kext = jnp.concatenate([kprev_ref[...], kcur_ref[...]], axis=0)
    vext = jnp.concatenate([vprev_ref[...], vcur_ref[...]], axis=0)
    lo = lax.broadcasted_iota(jnp.int32, (kext.shape[0], LANES), 1) < HEAD_DIM_A
    r_idx = lax.broadcasted_iota(jnp.int32, (rows_u, n_keys), 0)
    k_idx = lax.broadcasted_iota(jnp.int32, (rows_u, n_keys), 1)
    first_pair = (lax.broadcasted_iota(jnp.int32, (rows_u, 1), 0) // CHUNK) % 2 == 0
    ci = r_idx // (2 * CHUNK)
    band = (k_idx >= ci * CHUNK) & (k_idx < (ci + 3) * CHUNK)

    kv_ops = []
    for kv in range(N_KV_A):
        col, odd = kv // 2, kv % 2
        kc = kext[:, col * LANES:(col + 1) * LANES]
        vc = vext[:, col * LANES:(col + 1) * LANES]
        ks = pltpu.roll(kc, HEAD_DIM_A, axis=1)
        vs = pltpu.roll(vc, HEAD_DIM_A, axis=1)
        if odd:
            kc, ks, vc, vs = ks, kc, vs, vc
        kv_ops.append(((jnp.where(lo, kc, 0.0).astype(BF16), jnp.where(lo, vc, 0.0).astype(BF16)),
                       (jnp.where(lo, 0.0, ks).astype(BF16), jnp.where(lo, 0.0, vs).astype(BF16))))

    units = [(g, kv, half) for g in range(n_ch // gs) for kv in range(N_KV_A) for half in range(2)]
    st = {}

    def scores(g, kv, half):
        base = kv * GROUP_A * HEAD_DIM_A
        start = g * gs * CHUNK
        k_m = kv_ops[kv][half][0][start:start + n_keys]
        qst = jnp.concatenate(
            [q_ref[(g * gs + c) * CHUNK:(g * gs + c + 1) * CHUNK, base + pair * LANES:base + (pair + 1) * LANES]
             for c in range(gs) for pair in range(2)], axis=0)
        st["s", g, kv, half] = _dot_nt(qst, k_m)

    def softmax(g, kv, half):
        start = g * gs * CHUNK
        s = st.pop(("s", g, kv, half))
        ok = None
        if gs > 1:
            ok = band
        if not prev_valid and start < WINDOW:
            valid = k_idx + start >= WINDOW
            ok = valid if ok is None else ok & valid
        if ok is not None:
            s = jnp.where(ok, s, NEG_INF)
        sink = jnp.where(first_pair, sinks_ref[kv * GROUP_A + half], sinks_ref[kv * GROUP_A + 2 + half])
        m = jnp.maximum(jnp.max(s, axis=-1, keepdims=True), sink)
        p = jnp.exp2(s - m)
        den = jnp.sum(p, axis=-1, keepdims=True) + jnp.exp2(sink - m)
        st[g, kv, half] = (p.astype(BF16), 1.0 / den)

    def values(g, kv, half):
        base = kv * GROUP_A * HEAD_DIM_A
        start = g * gs * CHUNK
        p, rden = st.pop((g, kv, half))
        v_m = kv_ops[kv][half][1][start:start + n_keys]
        o = jnp.dot(p, v_m, preferred_element_type=F32) * rden
        if half == 0:
            st[g, kv] = o
            return
        acc = st.pop((g, kv)) + o
        for c in range(gs):
            rows = slice((g * gs + c) * CHUNK, (g * gs + c + 1) * CHUNK)
            for pair in range(2):
                sl = slice(base + pair * LANES, base + (pair + 1) * LANES)
                blk = acc[(2 * c + pair) * CHUNK:(2 * c + pair + 1) * CHUNK]
                o_ref[rows, sl] = (blk * gate_ref[rows, sl].astype(F32)).astype(o_ref.dtype)

    for thunk in _lagged([[functools.partial(fn, *u) for u in units] for fn in (scores, softmax, values)],
                         ATTN_LAG):
        thunk()


def _attn_kernel(*refs, n_ch, has_hist):
    if has_hist:
        _attn_body(*refs, n_ch=n_ch, prev_valid=True)
        return
    t = pl.program_id(1)

    @pl.when(t == 0)
    def _():
        _attn_body(*refs, n_ch=n_ch, prev_valid=False)

    @pl.when(t > 0)
    def _():
        _attn_body(*refs, n_ch=n_ch, prev_valid=True)


def _attention(sinks2, q, k, v, gate, k_hist, v_hist, bsz, n_c, n_ch):
    n = q.shape[0]
    n_t = n_c // n_ch
    tok = n_ch * CHUNK
    cur = lambda w: pl.BlockSpec((tok, w), lambda b, t: (b * n_t + t, 0))
    if k_hist is None:
        per_seq = n_c * CHUNK // WINDOW
        prev = pl.BlockSpec((WINDOW, KV_WIDTH_A),
                            lambda b, t: (b * per_seq + jnp.maximum(t * (tok // WINDOW) - 1, 0), 0))
        kp, vp = k, v
    else:
        assert n_t == 1
        prev = pl.BlockSpec((WINDOW, KV_WIDTH_A), lambda b, t: (b, 0))
        kp, vp = k_hist, v_hist
    return pl.pallas_call(
        functools.partial(_attn_kernel, n_ch=n_ch, has_hist=k_hist is not None),
        grid=(bsz, n_t),
        in_specs=[pl.BlockSpec(memory_space=pltpu.SMEM),
                  cur(WIDTH_A), prev, cur(KV_WIDTH_A), prev, cur(KV_WIDTH_A), cur(WIDTH_A)],
        out_specs=cur(WIDTH_A),
        out_shape=jax.ShapeDtypeStruct((n, WIDTH_A), BF16),
        compiler_params=pltpu.CompilerParams(dimension_semantics=("parallel", "parallel"),
                                             vmem_limit_bytes=VMEM_LIMIT),
        name="band_attention",
    )(sinks2, q, kp, k, vp, v, gate)


def _lane_col(x, lane_idx, j):
    return jnp.sum(jnp.where(lane_idx == j, x, 0.0), axis=-1, keepdims=True)


def _delta_kernel(*refs, has_state, n_ch):
    (cs_ref, ab_ref, gate_ref, alog_ref, dtb_ref, nw_ref,
     oa_ref, sga_ref, sgb_ref, x_ref, woa_ref, wob_ref, wout_ref, lng_ref, lnb_ref) = refs[:15]
    if has_state:
        s0_ref, y_ref, s_ref, ob_ref = refs[15:]
    else:
        y_ref, s_ref, ob_ref = refs[15:]

    @pl.when(pl.program_id(1) == 0)
    def _():
        if has_state:
            s_ref[...] = s0_ref[...]
        else:
            s_ref[...] = jnp.zeros_like(s_ref)

    ri = lax.broadcasted_iota(jnp.int32, (CHUNK, LANES), 0)
    li = lax.broadcasted_iota(jnp.int32, (CHUNK, LANES), 1)
    ci = li % CHUNK
    lo = li < CHUNK
    lo_row = lax.broadcasted_iota(jnp.int32, (1, LANES), 1) < CHUNK
    causal = ri >= ci
    strict = ri > ci
    eye2 = jnp.where(ri == ci, 1.0, 0.0).astype(F32)
    si = lax.broadcasted_iota(jnp.int32, (LANES, LANES), 0)
    sl = lax.broadcasted_iota(jnp.int32, (LANES, LANES), 1)
    cum2 = jnp.where((si < CHUNK) & (si <= sl % CHUNK), 1.0, 0.0).astype(F32)
    last_lane = lax.broadcasted_iota(jnp.int32, (N_HEADS_B, LANES), 1) == CHUNK - 1
    heads = range(N_HEADS_B)
    pairs = range(N_HEADS_B // 2)
    zero_rhs = jnp.zeros((CHUNK, 2 * HEAD_DV_B), BF16)
    zero_v = jnp.zeros((CHUNK, HEAD_DV_B), BF16)
    zero_k = jnp.zeros((CHUNK, HEAD_DK_B), BF16)
    ops = {(c, p): {} for c in range(n_ch) for p in pairs}
    gates = [None] * n_ch

    def blockdiag(x):
        return jnp.concatenate([jnp.where(lo, x, 0.0), jnp.where(lo, 0.0, x)], axis=0).astype(BF16)

    def gate_stage(c):
        ab = ab_ref[c * CHUNK:(c + 1) * CHUNK, :]
        ab_t = jnp.concatenate([ab, ab], axis=0).T
        g_t = -jnp.exp(alog_ref[...]) * _softplus(ab_t[0:N_HEADS_B] + dtb_ref[...])
        gc_t = jnp.dot(g_t, cum2, preferred_element_type=F32, precision=lax.Precision.HIGHEST)
        gl = jnp.sum(jnp.where(last_lane, gc_t, 0.0), axis=-1, keepdims=True)
        pad = jnp.zeros((LANES - 3 * N_HEADS_B, LANES), F32)
        packed = jnp.concatenate([gc_t, jnp.exp(gc_t), -_sigmoid(ab_t[N_HEADS_B:2 * N_HEADS_B]), pad],
                                 axis=0)
        gates[c] = dict(gc_t=gc_t, cols=packed.T[:CHUNK], e_rev_t=jnp.exp(gl - gc_t),
                        e_gl=jnp.broadcast_to(jnp.exp(gl), (N_HEADS_B, LANES)))

    def prep(c, p):
        gt, op = gates[c], ops[c, p]
        rows = slice(c * CHUNK, (c + 1) * CHUNK)
        slab = lambda j: cs_ref[rows, j * LANES:(j + 1) * LANES].astype(F32)
        per = []
        for h in (2 * p, 2 * p + 1):
            col = lambda i: _lane_col(gt["cols"], li, N_HEADS_B * i + h)
            q, k, v = slab(h), slab(N_HEADS_B + h), slab(2 * N_HEADS_B + h)
            q = q * (lax.rsqrt(jnp.sum(q * q, axis=-1, keepdims=True) + L2_EPS) * (HEAD_DK_B ** -0.5))
            k = k * lax.rsqrt(jnp.sum(k * k, axis=-1, keepdims=True) + L2_EPS)
            gc_col, e_gc, nb = col(0), col(1), col(2)
            kbn = k * nb
            per.append(dict(
                lhs=jnp.concatenate([kbn.astype(BF16), q.astype(BF16)], axis=0),
                k=k.astype(BF16),
                rhs_uw=jnp.concatenate([(v * -nb).astype(BF16), (kbn * e_gc).astype(BF16)], axis=1),
                q_dec=(q * e_gc).astype(BF16), k_f32=k, gc_col=gc_col,
                e_gl=gt["e_gl"][h:h + 1, :]))
        a, b = per
        pair_row = lambda t: jnp.where(lo_row, t[2 * p:2 * p + 1, :], t[2 * p + 1:2 * p + 2, :])
        gc_row = pair_row(gt["gc_t"])
        diff = jnp.where(lo, a["gc_col"], b["gc_col"]) - gc_row
        op.update(
            lhs=jnp.concatenate([a["lhs"], b["lhs"]], axis=1),
            k_bd=jnp.concatenate([jnp.concatenate([a["k"], zero_k], axis=1),
                                  jnp.concatenate([zero_k, b["k"]], axis=1)], axis=0),
            rhs_uw=(jnp.concatenate([a["rhs_uw"], zero_rhs], axis=0),
                    jnp.concatenate([zero_rhs, b["rhs_uw"]], axis=0)),
            q_dec=(a["q_dec"], b["q_dec"]),
            k_dec_t=(jnp.concatenate([a["k_f32"], b["k_f32"]], axis=0).T * pair_row(gt["e_rev_t"])).astype(BF16),
            decay=jnp.where(causal, jnp.exp(jnp.where(causal, diff, 0.0)), 0.0),
            e_gl=(a["e_gl"], b["e_gl"]))

    def scores(c, p):
        op = ops[c, p]
        aq = _dot_nt(op["lhs"], op["k_bd"])
        op["p"] = jnp.where(strict, aq[:CHUNK] * op["decay"], 0.0)
        op["qk"] = (aq[CHUNK:] * op["decay"]).astype(BF16)

    def inv_first(c, p):
        op = ops[c, p]
        op["t"] = eye2 + op["p"]
        op["p"] = jnp.dot(op["p"].astype(BF16), blockdiag(op["p"]), preferred_element_type=F32)

    def inv_mid(c, p):
        op = ops[c, p]
        r = jnp.dot(jnp.concatenate([op["p"].astype(BF16), op["t"].astype(BF16)], axis=0), blockdiag(op["p"]),
                    preferred_element_type=F32)
        op["p"] = r[:CHUNK]
        op["t"] = op["t"] + r[CHUNK:]

    def inv_last(c, p):
        op = ops[c, p]
        op["t"] = op["t"] + jnp.dot(op["t"].astype(BF16), blockdiag(op["p"]), preferred_element_type=F32)

    def apply_inv(c, p):
        op = ops[c, p]
        tb = op["t"].astype(BF16)
        op["u"], op["wq"] = [], []
        for j in range(2):
            uw = jnp.dot(tb, op["rhs_uw"][j], preferred_element_type=F32)
            op["u"].append(uw[:, :HEAD_DV_B])
            op["wq"].append(jnp.concatenate([uw[:, HEAD_DV_B:].astype(BF16), op["q_dec"][j]], axis=0))

    state = [None] * N_HEADS_B

    def load_state(h):
        state[h] = s_ref[0, h]

    def rec_first(c, h):
        op, j = ops[c, h // 2], h % 2
        ws = _dot(op["wq"][j], state[h])
        op["v_new", j] = (op["u"][j] + ws[:CHUNK]).astype(BF16)
        op["qs", j] = ws[CHUNK:]

    def rec_second(c, p):
        op = ops[c, p]
        rows = slice(c * CHUNK, (c + 1) * CHUNK)
        v_bd = jnp.concatenate([jnp.concatenate([op["v_new", 0], zero_v], axis=1),
                                jnp.concatenate([zero_v, op["v_new", 1]], axis=1)], axis=0)
        r = jnp.dot(jnp.concatenate([op["qk"], op["k_dec_t"]], axis=0), v_bd,
                    preferred_element_type=F32)
        for j in range(2):
            h = 2 * p + j
            hs = slice(h * HEAD_DV_B, (h + 1) * HEAD_DV_B)
            rj = r[:, j * HEAD_DV_B:(j + 1) * HEAD_DV_B]
            o = op["qs", j] + rj[:CHUNK]
            state[h] = state[h] * op["e_gl"][j] + rj[CHUNK:]
            o = o * lax.rsqrt(jnp.mean(o * o, axis=-1, keepdims=True) + RMS_EPS) * nw_ref[...]
            ob_ref[rows, hs] = (o * gate_ref[rows, hs].astype(F32)).astype(ob_ref.dtype)

    col_blocks = [slice(j * OUT_COLS, (j + 1) * OUT_COLS) for j in range(D_MODEL // OUT_COLS)]
    outs = {}

    def mix(chunks, cb):
        rows = slice(chunks[0] * CHUNK, (chunks[-1] + 1) * CHUNK)
        y_a = jnp.dot(oa_ref[rows, :], woa_ref[:, cb], preferred_element_type=F32)
        y_b = jnp.dot(ob_ref[rows, :], wob_ref[:, cb], preferred_element_type=F32)
        hmix = sga_ref[rows, cb].astype(F32) * y_a + sgb_ref[rows, cb].astype(F32) * y_b
        outs.setdefault(chunks[0], []).append(hmix.astype(BF16))

    def join(chunks):
        outs[chunks[0]] = jnp.concatenate(outs[chunks[0]], axis=1)

    def project(chunks, cb):
        outs.setdefault(("sub", chunks[0]), []).append(
            jnp.dot(outs[chunks[0]], wout_ref[:, cb], preferred_element_type=F32))

    def norm(chunks):
        rows = slice(chunks[0] * CHUNK, (chunks[-1] + 1) * CHUNK)
        del outs[chunks[0]]
        r = ALPHA * x_ref[rows, :] + jnp.concatenate(outs.pop(("sub", chunks[0])), axis=1)
        mu = jnp.mean(r, axis=-1, keepdims=True)
        d = r - mu
        var = jnp.mean(d * d, axis=-1, keepdims=True)
        y_ref[rows, :] = d * lax.rsqrt(var + LN_EPS) * lng_ref[...] + lnb_ref[...]

    def output_stream(chunks):
        span = OUT_GROUPS * len(chunks)
        if (chunks[-1] + 1) % span and chunks[-1] + 1 != n_ch:
            return []
        first = chunks[-1] + 1 - span if (chunks[-1] + 1) % span == 0 else (chunks[-1] // span) * span
        chunks = list(range(first, chunks[-1] + 1))
        return ([functools.partial(mix, chunks, cb) for cb in col_blocks] + [functools.partial(join, chunks)]
                + [functools.partial(project, chunks, cb) for cb in col_blocks] + [functools.partial(norm, chunks)])

    def over(fn, chunks, idx):
        return [functools.partial(fn, c, i) for c in chunks for i in idx]

    def prep_stream(chunks):
        return [functools.partial(gate_stage, c) for c in chunks] + over(prep, chunks, pairs)

    def level_stream(levels):
        return lambda chunks: [t for fn in levels for t in over(fn, chunks, pairs)]

    def recurrence_stream(chunks):
        return [t for c in chunks for t in over(rec_first, [c], heads) + over(rec_second, [c], pairs)]

    stages = [prep_stream,
              level_stream([scores, inv_first, inv_mid]),
              level_stream([inv_mid, inv_mid, inv_mid]),
              level_stream([inv_last, apply_inv]),
              recurrence_stream,
              output_stream]
    group = DELTA_GROUP if n_ch % DELTA_GROUP == 0 else 1
    groups = [list(range(g * group, (g + 1) * group)) for g in range(n_ch // group)]
    for h in heads:
        load_state(h)
    for step in range(len(groups) + len(stages) - 1):
        streams = [stage(groups[step - k]) for k, stage in reversed(list(enumerate(stages)))
                   if 0 <= step - k < len(groups)]
        for thunk in _interleave(streams):
            thunk()
    for h in heads:
        s_ref[0, h] = state[h]


def _delta_out(cs, ab, gate, alog_col, dtb_col, nw_row, oa, sga, sgb, x2d, woa, wob, wout, lng, lnb, s0,
               bsz, n_c, n_ch):
    n = cs.shape[0]
    n_t = n_c // n_ch
    cur = lambda w: pl.BlockSpec((n_ch * CHUNK, w), lambda b, t: (b * n_t + t, 0))
    const = lambda shape: pl.BlockSpec(shape, lambda b, t: (0,) * len(shape))
    weight = pl.BlockSpec((D_MODEL, D_MODEL), lambda b, t: (0, 0), pipeline_mode=pl.Buffered(1))
    state = pl.BlockSpec((1, N_HEADS_B, HEAD_DK_B, HEAD_DV_B), lambda b, t: (b, 0, 0, 0))
    in_specs = [cur(CONV_CH), cur(AB_PAD), cur(WIDTH_B),
                const((N_HEADS_B, 1)), const((N_HEADS_B, 1)), const((1, HEAD_DV_B)),
                cur(WIDTH_A), cur(D_MODEL), cur(D_MODEL), cur(D_MODEL), weight, weight, weight,
                const((1, D_MODEL)), const((1, D_MODEL))]
    args = [cs, ab, gate, alog_col, dtb_col, nw_row, oa, sga, sgb, x2d, woa, wob, wout, lng, lnb]
    has_state = s0 is not None
    if has_state:
        in_specs.append(state)
        args.append(s0)
    return pl.pallas_call(
        functools.partial(_delta_kernel, has_state=has_state, n_ch=n_ch),
        grid=(bsz, n_t),
        in_specs=in_specs,
        out_specs=[cur(D_MODEL), state],
        out_shape=[jax.ShapeDtypeStruct((n, D_MODEL), F32),
                   jax.ShapeDtypeStruct((bsz, N_HEADS_B, HEAD_DK_B, HEAD_DV_B), F32)],
        scratch_shapes=[pltpu.VMEM((n_ch * CHUNK, WIDTH_B), BF16)],
        compiler_params=pltpu.CompilerParams(dimension_semantics=("parallel", "arbitrary"),
                                             vmem_limit_bytes=VMEM_LIMIT),
        name="gated_delta_out",
    )(*args)


def _rope_tables(pos):
    half = ROT_DIM // 2
    inv_freq = jnp.power(ROPE_THETA, -jnp.arange(half, dtype=F32) / half)
    ang = pos[:, None] * inv_freq[None, :]
    cos, sin = jnp.cos(ang), jnp.sin(ang)
    n = pos.shape[0]
    ones = jnp.ones((n, HEAD_DIM_A - ROT_DIM), F32)
    zeros = jnp.zeros((n, HEAD_DIM_A - ROT_DIM), F32)
    zh = jnp.zeros((n, half), F32)
    ct = jnp.concatenate([cos, cos, ones], axis=1)
    sa = jnp.concatenate([zh, sin, zeros], axis=1)
    sb = jnp.concatenate([-sin, zh, zeros], axis=1)
    tile2 = lambda t: jnp.concatenate([t, t], axis=1)
    return tile2(ct), tile2(sa), tile2(sb)


def _layer(x, pos_offset, k_hist, v_hist, conv_hist, s0, weights):
    (w_r, sinks2, conv_w, alog_col, dtb_col, nw_row, woa, wob, wout, lng, lnb) = weights
    bsz, l_len, _ = x.shape
    n = bsz * l_len
    n_c = l_len // CHUNK
    n_ch = min(ATTN_TILE, l_len) // CHUNK
    x2d = x.reshape(n, D_MODEL)
    pos = jnp.arange(l_len, dtype=F32) + pos_offset
    tables = _rope_tables(pos)
    if k_hist is None:
        tm = min(PROJ_TILE, l_len)
        kh = vh = hist8 = None
    else:
        tm = min(PROJ_TILE, n)
        assert tm % l_len == 0 and l_len >= SUBLANES
        tables = [jnp.tile(t, (tm // l_len, 1)) for t in tables]
        kh = k_hist.reshape(bsz * WINDOW, KV_WIDTH_A)
        vh = v_hist.reshape(bsz * WINDOW, KV_WIDTH_A)
        hist8 = jnp.pad(conv_hist, ((0, 0), (SUBLANES - (CONV_W - 1), 0), (0, 0)))
    q, k, v, gza, cs, gzb, sga, sgb, ab, tail = _projection(x2d, hist8, w_r, *tables, conv_w, tm, l_len)
    oa = _attention(sinks2, q, k, v, gza, kh, vh, bsz, n_c, n_ch)
    y, s_new = _delta_out(cs, ab, gzb, alog_col, dtb_col, nw_row, oa, sga, sgb, x2d, woa, wob, wout, lng, lnb, s0,
                          bsz, n_c, min(DELTA_TILE, l_len) // CHUNK)

    def split_heads(t):
        return jnp.stack([t[..., j * HEAD_DIM_A:(j + 1) * HEAD_DIM_A] for j in range(N_KV_A)], axis=2)

    k3 = k.reshape(bsz, l_len, KV_WIDTH_A)
    v3 = v.reshape(bsz, l_len, KV_WIDTH_A)
    if k_hist is None:
        new_k, new_v = split_heads(k3[:, l_len - WINDOW:]), split_heads(v3[:, l_len - WINDOW:])
    else:
        keep = k_hist.shape[1]
        new_k = jnp.concatenate([k_hist[:, l_len:], split_heads(k3)], axis=1)
        new_v = jnp.concatenate([v_hist[:, l_len:], split_heads(v3)], axis=1)
        assert new_k.shape[1] == keep == WINDOW
    new_conv = tail.reshape(bsz, -1, SUBLANES, CONV_CH)[:, -1, SUBLANES - (CONV_W - 1):]
    return y.reshape(bsz, l_len, D_MODEL), new_k, new_v, new_conv, s_new


def kernel(x_prompt, x_sample, cache_attn_k, cache_attn_v, state_conv, state_delta, w_in, attn_sinks, conv_w,
           a_log, dt_bias, delta_norm_w, w_o_attn, w_o_delta, w_out, ln_g, ln_b):
    splits = np.cumsum((WIDTH_A, KV_WIDTH_A, KV_WIDTH_A, WIDTH_A, CONV_CH, N_HEADS_B, N_HEADS_B, WIDTH_B,
                        D_MODEL, D_MODEL))
    ab_lo, ab_hi = int(splits[4]), int(splits[6])
    w_r = jnp.concatenate([w_in[:, :ab_lo], w_in[:, ab_hi:], w_in[:, ab_lo:ab_hi],
                           jnp.zeros((D_MODEL, AB_PAD - (ab_hi - ab_lo)), w_in.dtype)], axis=1).astype(BF16)
    head_col = lambda t: t.astype(F32).reshape(N_HEADS_B, 1)
    weights = (w_r, attn_sinks.astype(F32) * LOG2E, conv_w.astype(F32), head_col(a_log), head_col(dt_bias),
               delta_norm_w.astype(F32).reshape(1, HEAD_DV_B), w_o_attn.astype(BF16), w_o_delta.astype(BF16),
               w_out.astype(BF16), ln_g.astype(F32).reshape(1, D_MODEL), ln_b.astype(F32).reshape(1, D_MODEL))
    yp, kp, vp, cp, sp = _layer(x_prompt, 0.0, None, None, None, None, weights)
    ys, ks, vs, cs, ss = _layer(x_sample, float(PAST_LEN), cache_attn_k, cache_attn_v, state_conv, state_delta,
                                weights)
    return (yp, ys, kp, vp, cp, sp, ks, vs, cs, ss)
```
